```python
import math, functools
import jax, jax.numpy as jnp
from jax import lax
import numpy as np

D_MODEL = 1024
BATCH = 4
SEQ = 8192
DEPTH = 2
DEC_BATCH = 32
DEC_SEQ = 1
PAST_LEN = 16384
PAGE_SIZE = 128

N_EVEN = (DEPTH + 1) // 2
N_ODD = DEPTH // 2
D_A = D_MODEL // 2
CONV_A = 31
DH_B = 64
DV_B = 2 * DH_B
H_B = (D_MODEL // 2) // DV_B
D_QK = H_B * 2 * DH_B
D_BO = H_B * DV_B
D_C = D_MODEL // 2
POOL_WINDOWS = (2, 4, 8, 16)
N_POOL_GROUPS = len(POOL_WINDOWS)
GC = D_C // N_POOL_GROUPS
POOL_HIST = max(POOL_WINDOWS) - 1
D_D = D_MODEL // 2
CONV_D = 3
W_IN_EVEN = 2 * D_A + 2 * D_QK + D_BO
W_IN_ODD = D_C + 3 * D_D
D_FF = ((8 * D_MODEL // 3 + 255) // 256) * 256
N_SUB = 3
Q_BLOCK = 128
EPS = 1e-6

kernel_name = 'hybrid_convattn_poolconv_decoder_step'


def rmsnorm(x, g):
    xf = x.astype(jnp.float32)
    y = xf * lax.rsqrt(jnp.mean(xf * xf, axis=-1, keepdims=True) + EPS)
    return (y * g.astype(jnp.float32)).astype(x.dtype)


def layernorm(x, g, b):
    xf = x.astype(jnp.float32)
    mu = jnp.mean(xf, axis=-1, keepdims=True)
    var = jnp.mean(jnp.square(xf - mu), axis=-1, keepdims=True)
    y = (xf - mu) * lax.rsqrt(var + EPS)
    return (y * g.astype(jnp.float32) + b.astype(jnp.float32)).astype(x.dtype)


def causal_dwconv(z_full, w):
    c = z_full.shape[-1]
    return lax.conv_general_dilated(z_full, w[:, None, :].astype(z_full.dtype), (1,), 'VALID',
                                    dimension_numbers=('NWC', 'WIO', 'NWC'), feature_group_count=c)


def swiglu(x, w_in, w_out):
    g, u = jnp.split(x @ w_in, 2, axis=-1)
    return (jax.nn.silu(g) * u) @ w_out


def lambda_init(layer):
    return 0.8 - 0.6 * math.exp(-0.3 * layer)


def diff_lambda(lam_qk, lam_init):
    lf = lam_qk.astype(jnp.float32)
    return jnp.exp(jnp.sum(lf[0] * lf[1])) - jnp.exp(jnp.sum(lf[2] * lf[3])) + lam_init


def diff_attn_prompt(q, k, v, lam):
    b, l = q.shape[:2]
    nb = l // Q_BLOCK
    qb = jnp.moveaxis(q.reshape(b, nb, Q_BLOCK, H_B, 2, DH_B), 1, 0)
    kpos = jnp.arange(l)
    scale = DH_B ** -0.5

    def block(args):
        qi, bi = args
        s = jnp.einsum('bqhcd,bkhcd->bhcqk', qi, k, preferred_element_type=jnp.float32) * scale
        qpos = bi * Q_BLOCK + jnp.arange(Q_BLOCK)
        mask = kpos[None, :] <= qpos[:, None]
        p = jax.nn.softmax(jnp.where(mask, s, -jnp.inf), axis=-1)
        a = p[:, :, 0] - lam * p[:, :, 1]
        return jnp.einsum('bhqk,bkhd->bqhd', a.astype(v.dtype), v)

    o = lax.map(block, (qb, jnp.arange(nb)))
    return jnp.moveaxis(o, 0, 1).reshape(b, l, H_B, DV_B)


def diff_attn_sample(q, k, v, lam, k_past, v_past):
    lq = q.shape[1]
    p_len = k_past.shape[1]
    scale = DH_B ** -0.5
    s_past = jnp.einsum('bqhcd,bkhcd->bhcqk', q, k_past, preferred_element_type=jnp.float32) * scale
    s_new = jnp.einsum('bqhcd,bkhcd->bhcqk', q, k, preferred_element_type=jnp.float32) * scale
    causal = jnp.arange(lq)[None, :] <= jnp.arange(lq)[:, None]
    s_new = jnp.where(causal, s_new, -jnp.inf)
    p = jax.nn.softmax(jnp.concatenate([s_past, s_new], axis=-1), axis=-1)
    a = (p[:, :, 0] - lam * p[:, :, 1]).astype(v.dtype)
    return (jnp.einsum('bhqk,bkhd->bqhd', a[..., :p_len], v_past)
            + jnp.einsum('bhqk,bkhd->bqhd', a[..., p_len:], v))


def multiscale_pool(u, hist, start_pos, w_pool, pool_scale):
    b, l, _ = u.shape
    full = jnp.concatenate([hist.astype(u.dtype), u], axis=1)
    cs = jnp.cumsum(full.astype(jnp.float32), axis=1)
    cs = jnp.concatenate([jnp.zeros((b, 1, D_C), jnp.float32), cs], axis=1)
    pos = start_pos + jnp.arange(l)
    means = []
    for g, w in enumerate(POOL_WINDOWS):
        grp = slice(g * GC, (g + 1) * GC)
        win = cs[:, POOL_HIST + 1:, grp] - cs[:, POOL_HIST + 1 - w:POOL_HIST + 1 - w + l, grp]
        cnt = jnp.minimum(pos + 1, w).astype(jnp.float32)
        means.append(win / cnt[None, :, None])
    m = (jnp.concatenate(means, axis=-1) - u.astype(jnp.float32)).reshape(b, l, N_POOL_GROUPS, GC)
    y = jnp.einsum('blgc,gcd->blgd', m, w_pool.astype(jnp.float32)).reshape(b, l, D_C)
    y = y * pool_scale.astype(jnp.float32)
    return y.astype(u.dtype), full[:, -POOL_HIST:]


def even_mixer(h, conv_hist, attend, lam_init, w_in, w_out, conv_w, conv_b, ln_g, ln_b, lam_qk, subln_g):
    b, l, _ = h.shape
    proj = h @ w_in
    a_val, a_gate, q, k, v = jnp.split(proj, [D_A, 2 * D_A, 2 * D_A + D_QK, 2 * D_A + 2 * D_QK], axis=-1)
    u = a_val * jax.nn.sigmoid(a_gate)
    u_full = jnp.concatenate([conv_hist.astype(u.dtype), u], axis=1)
    a = causal_dwconv(u_full, conv_w) + conv_b
    a = jax.nn.silu(layernorm(a, ln_g, ln_b))
    q = q.reshape(b, l, H_B, 2, DH_B)
    k = k.reshape(b, l, H_B, 2, DH_B)
    v = v.reshape(b, l, H_B, DV_B)
    lam = diff_lambda(lam_qk, lam_init)
    o = attend(q, k, v, lam)
    o = rmsnorm(o, subln_g) * (1.0 - lam_init)
    out = jnp.concatenate([a, o.reshape(b, l, D_BO)], axis=-1) @ w_out
    return out, u_full[:, -(CONV_A - 1):], k.reshape(b, l, H_B, 2 * DH_B), v


def odd_mixer(h, pool_hist, convd_hist, start_pos, w_in, w_out, w_pool, pool_scale, conv_w):
    proj = h @ w_in
    u, bg, cg, hv = jnp.split(proj, [D_C, D_C + D_D, D_C + 2 * D_D], axis=-1)
    pc, new_pool = multiscale_pool(u, pool_hist, start_pos, w_pool, pool_scale)
    z = cg * hv
    z_full = jnp.concatenate([convd_hist.astype(z.dtype), z], axis=1)
    y = bg * causal_dwconv(z_full, conv_w)
    out = jnp.concatenate([pc, y], axis=-1) @ w_out
    return out, new_pool, z_full[:, -(CONV_D - 1):]


def run_trunk(x, c, start_pos, attend, conv_a_hist, pool_hist, convd_hist, p):
    b = x.shape[0]
    ks, vs, cas, pls, cds = [], [], [], [], []
    for i in range(DEPTH):
        mod = jax.nn.silu(c.astype(jnp.float32)) @ p['w_ada'][i].astype(jnp.float32) + p['b_ada'][i].astype(jnp.float32)
        mod = mod.reshape(b, N_SUB, 3, 1, D_MODEL).astype(x.dtype)

        def pre(z, s):
            return rmsnorm(z, p['norm_g'][i, s, 0]) * (1 + mod[:, s, 1]) + mod[:, s, 0]

        def post(z, o, s, resw):
            return z + resw * (1 + mod[:, s, 2]) * rmsnorm(o, p['norm_g'][i, s, 1])

        x = post(x, swiglu(pre(x, 0), p['w_ffn_in'][i, 0], p['w_ffn_out'][i, 0]), 0, 0.5)
        if i % 2 == 0:
            e = i // 2
            o, ca, kr, vr = even_mixer(pre(x, 1), conv_a_hist[e], functools.partial(attend, e), lambda_init(i),
                                       p['w_in_even'][e], p['w_out_even'][e], p['conv_a_w'][e], p['conv_a_b'][e],
                                       p['ln_a_g'][e], p['ln_a_b'][e], p['lam_qk'][e], p['subln_g'][e])
            ks.append(kr); vs.append(vr); cas.append(ca)
        else:
            od = i // 2
            o, pl, cd = odd_mixer(pre(x, 1), pool_hist[od], convd_hist[od], start_pos,
                                  p['w_in_odd'][od], p['w_out_odd'][od], p['w_pool'][od], p['pool_scale'][od],
                                  p['conv_d_w'][od])
            pls.append(pl); cds.append(cd)
        x = post(x, o, 1, 1.0)
        x = post(x, swiglu(pre(x, 2), p['w_ffn_in'][i, 1], p['w_ffn_out'][i, 1]), 2, 0.5)
    return x, jnp.stack(ks), jnp.stack(vs), jnp.stack(cas), jnp.stack(pls), jnp.stack(cds)


def setup_inputs(seed: int = 0) -> dict:
    key = jax.random.key(seed)
    ks = iter(jax.random.split(key, 40))
    f32 = jnp.float32

    def nrm(shape, scale):
        return jax.random.normal(next(ks), shape, f32) * scale

    n_pages = PAST_LEN // PAGE_SIZE
    n_used = DEC_BATCH * n_pages
    n_pool = n_used + max(1, n_used // 4)
    page_table = jax.random.permutation(next(ks), n_pool)[:n_used].reshape(DEC_BATCH, n_pages).astype(jnp.int32)
    return {
        'x_prompt': nrm((BATCH, SEQ, D_MODEL), 1.0),
        'x_sample': nrm((DEC_BATCH, DEC_SEQ, D_MODEL), 1.0),
        'cache_k': nrm((N_EVEN, n_pool, PAGE_SIZE, H_B, 2 * DH_B), 1.0),
        'cache_v': nrm((N_EVEN, n_pool, PAGE_SIZE, H_B, DV_B), 1.0),
        'state_conv_a': nrm((N_EVEN, DEC_BATCH, CONV_A - 1, D_A), 0.5),
        'state_pool': nrm((N_ODD, DEC_BATCH, POOL_HIST, D_C), 1.0),
        'state_conv_d': nrm((N_ODD, DEC_BATCH, CONV_D - 1, D_D), 0.5),
        'page_table': page_table,
        'c_prompt': nrm((BATCH, D_MODEL), 1.0),
        'c_sample': nrm((DEC_BATCH, D_MODEL), 1.0),
        'w_ada': nrm((DEPTH, D_MODEL, N_SUB * 3 * D_MODEL), 0.2 * D_MODEL ** -0.5),
        'b_ada': nrm((DEPTH, N_SUB * 3 * D_MODEL), 0.02),
        'norm_g': 1.0 + nrm((DEPTH, N_SUB, 2, D_MODEL), 0.05),
        'w_ffn_in': nrm((DEPTH, 2, D_MODEL, 2 * D_FF), D_MODEL ** -0.5),
        'w_ffn_out': nrm((DEPTH, 2, D_FF, D_MODEL), D_FF ** -0.5),
        'w_in_even': nrm((N_EVEN, D_MODEL, W_IN_EVEN), D_MODEL ** -0.5),
        'w_out_even': nrm((N_EVEN, D_A + D_BO, D_MODEL), (D_A + D_BO) ** -0.5),
        'conv_a_w': nrm((N_EVEN, CONV_A, D_A), CONV_A ** -0.5),
        'conv_a_b': nrm((N_EVEN, D_A), 0.05),
        'ln_a_g': 1.0 + nrm((N_EVEN, D_A), 0.05),
        'ln_a_b': nrm((N_EVEN, D_A), 0.05),
        'lam_qk': nrm((N_EVEN, 4, DH_B), 0.1),
        'subln_g': 1.0 + nrm((N_EVEN, DV_B), 0.05),
        'w_in_odd': nrm((N_ODD, D_MODEL, W_IN_ODD), D_MODEL ** -0.5),
        'w_out_odd': nrm((N_ODD, D_C + D_D, D_MODEL), (D_C + D_D) ** -0.5),
        'w_pool': nrm((N_ODD, N_POOL_GROUPS, GC, GC), GC ** -0.5),
        'pool_scale': 1.0 + nrm((N_ODD, D_C), 0.1),
        'conv_d_w': nrm((N_ODD, CONV_D, D_D), CONV_D ** -0.5),
    }


def reference(x_prompt, x_sample, cache_k, cache_v, state_conv_a, state_pool, state_conv_d, page_table,
              c_prompt, c_sample, w_ada, b_ada, norm_g, w_ffn_in, w_ffn_out, w_in_even, w_out_even,
              conv_a_w, conv_a_b, ln_a_g, ln_a_b, lam_qk, subln_g, w_in_odd, w_out_odd, w_pool,
              pool_scale, conv_d_w):
    params = {'w_ada': w_ada, 'b_ada': b_ada, 'norm_g': norm_g, 'w_ffn_in': w_ffn_in, 'w_ffn_out': w_ffn_out,
              'w_in_even': w_in_even, 'w_out_even': w_out_even, 'conv_a_w': conv_a_w, 'conv_a_b': conv_a_b,
              'ln_a_g': ln_a_g, 'ln_a_b': ln_a_b, 'lam_qk': lam_qk, 'subln_g': subln_g,
              'w_in_odd': w_in_odd, 'w_out_odd': w_out_odd, 'w_pool': w_pool, 'pool_scale': pool_scale,
              'conv_d_w': conv_d_w}
    db = x_sample.shape[0]
    past_len = page_table.shape[1] * PAGE_SIZE

    def attend_prompt(e, q, k, v, lam):
        return diff_attn_prompt(q, k, v, lam)

    def attend_sample(e, q, k, v, lam):
        k_past = cache_k[e][page_table].reshape(db, past_len, H_B, 2, DH_B)
        v_past = cache_v[e][page_table].reshape(db, past_len, H_B, DV_B)
        return diff_attn_sample(q, k, v, lam, k_past.astype(q.dtype), v_past.astype(v.dtype))

    bp = x_prompt.shape[0]
    zeros_a = jnp.zeros((N_EVEN, bp, CONV_A - 1, D_A), x_prompt.dtype)
    zeros_p = jnp.zeros((N_ODD, bp, POOL_HIST, D_C), x_prompt.dtype)
    zeros_d = jnp.zeros((N_ODD, bp, CONV_D - 1, D_D), x_prompt.dtype)
    y_prompt, k_prompt, v_prompt, conv_a_prompt, pool_prompt, conv_d_prompt = run_trunk(
        x_prompt, c_prompt, 0, attend_prompt, zeros_a, zeros_p, zeros_d, params)
    y_sample, k_sample, v_sample, conv_a_sample, pool_sample, conv_d_sample = run_trunk(
        x_sample, c_sample, past_len, attend_sample, state_conv_a, state_pool, state_conv_d, params)
    return (y_prompt, y_sample, k_prompt, v_prompt, k_sample, v_sample, conv_a_prompt, conv_a_sample,
            pool_prompt, pool_sample, conv_d_prompt, conv_d_sample)
```

```python
import functools
import math

import jax
import jax.numpy as jnp
from jax import lax
from jax.experimental import pallas as pl
from jax.experimental.pallas import tpu as pltpu

F32 = jnp.float32
BF16 = jnp.bfloat16

D_MODEL = 1024
DEPTH = 2
PAGE_SIZE = 128
D_A = D_MODEL // 2
CONV_A = 31
DH_B = 64
DV_B = 2 * DH_B
H_B = (D_MODEL // 2) // DV_B
D_QK = H_B * 2 * DH_B
D_BO = H_B * DV_B
D_C = D_MODEL // 2
POOL_WINDOWS = (2, 4, 8, 16)
GC = D_C // len(POOL_WINDOWS)
POOL_HIST = max(POOL_WINDOWS) - 1
D_D = D_MODEL // 2
CONV_D = 3
D_FF = ((8 * D_MODEL // 3 + 255) // 256) * 256
N_SUB = 3
EPS = 1e-6

SUBLANES = 8
LANES = 128
VMEM_LIMIT_BYTES = 56 * 1024 * 1024

TOKEN_TILE = 512
FF_CHUNK = 256
CONV_ROWS = 128
HALO_A = 32
HALO_P = 16
HALO_D = 8
ATT_TQ = 256
ATT_TK = 512
PAGES_PER_CHUNK = 8


def _lambda_init(layer):
    return 0.8 - 0.6 * math.exp(-0.3 * layer)


def _params(semantics):
    return pltpu.CompilerParams(dimension_semantics=semantics, vmem_limit_bytes=VMEM_LIMIT_BYTES)


def _rms(x, g):
    return x * lax.rsqrt(jnp.mean(x * x, axis=-1, keepdims=True) + EPS) * g


def _mod(ref):
    v = ref[...]
    return v[0] if v.ndim == 3 else v


def _pre(x, g, scale, shift):
    return _rms(x, g) * (1.0 + scale) + shift


def _silu(x):
    return x * jax.nn.sigmoid(x)


def _dot(a, b):
    return jnp.dot(a, b, preferred_element_type=F32)


def _resident(shape):
    return pl.BlockSpec(shape, lambda *_: (0,) * len(shape), pipeline_mode=pl.Buffered(1))


def _mod_spec(arr, tm):
    if arr.ndim == 3:
        return pl.BlockSpec((1, 1, arr.shape[-1]), lambda b, t: (b, 0, 0))
    return pl.BlockSpec((tm, arr.shape[-1]), lambda b, t: (0, 0))


def _ada_body(c_ref, w_ref, b_ref, o_ref):
    c = c_ref[...]
    o_ref[0] = _dot(_silu(c).astype(BF16), w_ref[0].astype(BF16)) + b_ref[0]


def _ada_call(c_all, w_ada, b_ada):
    rows, d = c_all.shape
    depth, _, n = w_ada.shape
    tn = 1024
    return pl.pallas_call(
        _ada_body,
        out_shape=jax.ShapeDtypeStruct((depth, rows, n), F32),
        grid=(depth, n // tn),
        in_specs=[pl.BlockSpec((rows, d), lambda i, j: (0, 0)),
                  pl.BlockSpec((1, d, tn), lambda i, j: (i, 0, j)),
                  pl.BlockSpec((1, 1, tn), lambda i, j: (i, 0, j))],
        out_specs=pl.BlockSpec((1, rows, tn), lambda i, j: (i, 0, j)),
        compiler_params=_params(("parallel", "parallel")),
        name="ada_mod",
    )(c_all, w_ada, b_ada.reshape(depth, 1, n))


def _ffn_body(x_ref, sh_ref, sc_ref, gt_ref, gpre_ref, gpost_ref, win_ref, wout_ref, o_ref, act_ref,
              *, resw):
    x = x_ref[...]
    h = _pre(x, gpre_ref[...], _mod(sc_ref), _mod(sh_ref)).astype(BF16)
    for j in range(D_FF // FF_CHUNK):
        lo = j * FF_CHUNK
        g = _dot(h, win_ref[:, lo:lo + FF_CHUNK])
        u = _dot(h, win_ref[:, D_FF + lo:D_FF + lo + FF_CHUNK])
        act_ref[:, lo:lo + FF_CHUNK] = (_silu(g) * u).astype(BF16)
    o = _dot(act_ref[...], wout_ref[...])
    o_ref[...] = x + resw * (1.0 + _mod(gt_ref)) * _rms(o, gpost_ref[...])


def _ffn_call(x, sh, sc, gt, gpre, gpost, w_in, w_out, *, resw, nb, nt, tm, name):
    n, d = x.shape
    row = pl.BlockSpec((tm, d), lambda b, t: (b * nt + t, 0))
    return pl.pallas_call(
        functools.partial(_ffn_body, resw=resw),
        out_shape=jax.ShapeDtypeStruct((n, d), F32),
        grid=(nb, nt),
        in_specs=[row, _mod_spec(sh, tm), _mod_spec(sc, tm), _mod_spec(gt, tm),
                  _resident((1, d)), _resident((1, d)),
                  _resident(w_in.shape), _resident(w_out.shape)],
        out_specs=row,
        scratch_shapes=[pltpu.VMEM((tm, D_FF), BF16)],
        compiler_params=_params(("parallel", "parallel")),
        name=name,
    )(x, sh, sc, gt, gpre, gpost, w_in, w_out)


def _even_front_body(x_ref, sh_ref, sc_ref, gpre_ref, win_ref, cw_ref, cb_ref, lng_ref, lnb_ref,
                     a_ref, q_ref, kb_ref, vb_ref, k_ref, v_ref, st_ref, ubuf, *, tm):
    @pl.when(pl.program_id(1) == 0)
    def _():
        ubuf[0:HALO_A, :] = jnp.zeros((HALO_A, D_A), F32)

    x = x_ref[...]
    h = _pre(x, gpre_ref[...], _mod(sc_ref), _mod(sh_ref)).astype(BF16)

    def proj(c):
        return _dot(h, win_ref[:, c * D_A:(c + 1) * D_A])

    ubuf[HALO_A:HALO_A + tm, :] = proj(0) * jax.nn.sigmoid(proj(1))
    q_ref[...] = (proj(2) * (DH_B ** -0.5)).astype(BF16)
    k = proj(3)
    k_ref[...] = k
    kb_ref[...] = k.astype(BF16)
    v = proj(4)
    v_ref[...] = v
    vb_ref[...] = v.astype(BF16)

    first = HALO_A - (CONV_A - 1)
    for c in range(tm // CONV_ROWS):
        base = c * CONV_ROWS
        acc = jnp.broadcast_to(cb_ref[...], (CONV_ROWS, D_A))
        for j in range(CONV_A):
            lo = first + base + j
            acc = acc + cw_ref[j:j + 1, :] * ubuf[lo:lo + CONV_ROWS, :]
        mu = jnp.mean(acc, axis=-1, keepdims=True)
        cen = acc - mu
        var = jnp.mean(cen * cen, axis=-1, keepdims=True)
        y = cen * lax.rsqrt(var + EPS) * lng_ref[...] + lnb_ref[...]
        a_ref[base:base + CONV_ROWS, :] = _silu(y).astype(BF16)

    tail = ubuf[tm:tm + HALO_A, :]
    st_ref[0] = tail
    ubuf[0:HALO_A, :] = tail


def _even_front_call(x, sh, sc, gpre, w_in, cw, cb, lng, lnb, *, nb, nt, tm):
    n, d = x.shape
    row = lambda w: pl.BlockSpec((tm, w), lambda b, t: (b * nt + t, 0))
    out_shape = (jax.ShapeDtypeStruct((n, D_A), BF16),
                 jax.ShapeDtypeStruct((n, D_QK), BF16),
                 jax.ShapeDtypeStruct((n, D_QK), BF16),
                 jax.ShapeDtypeStruct((n, D_BO), BF16),
                 jax.ShapeDtypeStruct((n, D_QK), F32),
                 jax.ShapeDtypeStruct((n, D_BO), F32),
                 jax.ShapeDtypeStruct((nb, HALO_A, D_A), F32))
    return pl.pallas_call(
        functools.partial(_even_front_body, tm=tm),
        out_shape=out_shape,
        grid=(nb, nt),
        in_specs=[row(d), _mod_spec(sh, tm), _mod_spec(sc, tm), _resident((1, d)),
                  _resident(w_in.shape), _resident(cw.shape), _resident((1, D_A)),
                  _resident((1, D_A)), _resident((1, D_A))],
        out_specs=(row(D_A), row(D_QK), row(D_QK), row(D_BO), row(D_QK), row(D_BO),
                   pl.BlockSpec((1, HALO_A, D_A), lambda b, t: (b, 0, 0))),
        scratch_shapes=[pltpu.VMEM((HALO_A + tm, D_A), F32)],
        compiler_params=_params(("arbitrary", "arbitrary")),
        name="even_front_prompt",
    )(x, sh, sc, gpre, w_in, cw, cb, lng, lnb)


def _diff_lambda(lamqk_ref, lam_init):
    lq = lamqk_ref[...]
    s1 = jnp.sum(lq[0:1] * lq[1:2], axis=1, keepdims=True)
    s2 = jnp.sum(lq[2:3] * lq[3:4], axis=1, keepdims=True)
    return jnp.exp(s1) - jnp.exp(s2) + lam_init


def _attn_body(q_ref, k_ref, v_ref, lamqk_ref, g_ref, o_ref, m_ref, l_ref, acc_ref, *, lam_init, tq, tk):
    qi = pl.program_id(2)
    q = q_ref[...]
    lane = lax.broadcasted_iota(jnp.int32, q.shape, 1)
    zero = jnp.zeros_like(q)
    q2 = jnp.concatenate([jnp.where(lane < DH_B, q, zero), jnp.where(lane >= DH_B, q, zero)], axis=0)

    m_ref[...] = jnp.full(m_ref.shape, -jnp.inf, F32)
    l_ref[...] = jnp.zeros(l_ref.shape, F32)
    acc_ref[...] = jnp.zeros(acc_ref.shape, F32)

    def step(kt, masked):
        start = pl.multiple_of(kt * tk, tk)
        s = lax.dot_general(q2, k_ref[pl.ds(start, tk), :], (((1,), (1,)), ((), ())),
                            preferred_element_type=F32)
        if masked:
            qpos = qi * tq + lax.rem(lax.broadcasted_iota(jnp.int32, s.shape, 0), tq)
            kpos = start + lax.broadcasted_iota(jnp.int32, s.shape, 1)
            s = jnp.where(kpos <= qpos, s, -jnp.inf)
        m_old = m_ref[...]
        m_new = jnp.maximum(m_old, jnp.max(s, axis=-1, keepdims=True))
        alpha = jnp.exp(m_old - m_new)
        p = jnp.exp(s - m_new[:, 0:1])
        l_ref[...] = alpha * l_ref[...] + jnp.sum(p, axis=-1, keepdims=True)
        acc_ref[...] = alpha * acc_ref[...] + _dot(p.astype(BF16), v_ref[pl.ds(start, tk), :])
        m_ref[...] = m_new

    n_full = (qi * tq) // tk

    def full_step(kt, carry):
        step(kt, masked=False)
        return carry

    lax.fori_loop(0, n_full, full_step, 0)
    step(n_full, masked=True)

    lam = _diff_lambda(lamqk_ref, lam_init)
    n = acc_ref[...] / l_ref[...]
    o = n[0:tq] - lam * n[tq:2 * tq]
    o_ref[...] = (_rms(o, g_ref[...]) * (1.0 - lam_init)).astype(o_ref.dtype)


def _attn_call(q, k, v, lam_qk, g, *, nb, seq, lam_init):
    n = q.shape[0]
    nq = seq // ATT_TQ
    return pl.pallas_call(
        functools.partial(_attn_body, lam_init=lam_init, tq=ATT_TQ, tk=ATT_TK),
        out_shape=jax.ShapeDtypeStruct((n, D_BO), BF16),
        grid=(nb, H_B, nq),
        in_specs=[pl.BlockSpec((ATT_TQ, DV_B), lambda b, h, i: (b * nq + i, h)),
                  pl.BlockSpec((seq, DV_B), lambda b, h, i: (b, h)),
                  pl.BlockSpec((seq, DV_B), lambda b, h, i: (b, h)),
                  pl.BlockSpec(lam_qk.shape, lambda b, h, i: (0, 0)),
                  pl.BlockSpec((1, DV_B), lambda b, h, i: (0, 0))],
        out_specs=pl.BlockSpec((ATT_TQ, DV_B), lambda b, h, i: (b * nq + i, h)),
        scratch_shapes=[pltpu.VMEM((2 * ATT_TQ, LANES), F32),
                        pltpu.VMEM((2 * ATT_TQ, LANES), F32),
                        pltpu.VMEM((2 * ATT_TQ, DV_B), F32)],
        compiler_params=_params(("parallel", "parallel", "parallel")),
        name="diff_attn_prompt",
    )(q, k, v, lam_qk, g)


def _mixer_back_body(a_ref, b_ref, x_ref, gt_ref, gpost_ref, w_ref, o_ref):
    half = a_ref.shape[-1]
    r = (_dot(a_ref[...].astype(BF16), w_ref[0:half, :])
         + _dot(b_ref[...].astype(BF16), w_ref[half:2 * half, :]))
    o_ref[...] = x_ref[...] + (1.0 + _mod(gt_ref)) * _rms(r, gpost_ref[...])


def _mixer_back_call(a, b, x, gt, gpost, w_out, *, nb, nt, tm, name):
    n, d = x.shape
    row = lambda w: pl.BlockSpec((tm, w), lambda bb, t: (bb * nt + t, 0))
    return pl.pallas_call(
        _mixer_back_body,
        out_shape=jax.ShapeDtypeStruct((n, d), F32),
        grid=(nb, nt),
        in_specs=[row(a.shape[-1]), row(b.shape[-1]), row(d), _mod_spec(gt, tm),
                  _resident((1, d)), _resident(w_out.shape)],
        out_specs=row(d),
        compiler_params=_params(("parallel", "parallel")),
        name=name,
    )(a, b, x, gt, gpost, w_out)


def _odd_body(x_ref, sh_ref, sc_ref, gt_ref, gpre_ref, gpost_ref, win_ref, wout_ref, wpool_ref,
              pscale_ref, cw_ref, o_ref, pst_ref, dst_ref, pbuf, zbuf, mix_ref, *, tm, start_pos):
    t = pl.program_id(1)

    @pl.when(t == 0)
    def _():
        pbuf[0:HALO_P, :] = jnp.zeros((HALO_P, D_C), F32)
        zbuf[0:HALO_D, :] = jnp.zeros((HALO_D, D_D), F32)

    x = x_ref[...]
    h = _pre(x, gpre_ref[...], _mod(sc_ref), _mod(sh_ref)).astype(BF16)

    def proj(c):
        return _dot(h, win_ref[:, c * D_C:(c + 1) * D_C])

    u = proj(0)
    pbuf[HALO_P:HALO_P + tm, :] = u
    bg = proj(1)
    zbuf[HALO_D:HALO_D + tm, :] = proj(2) * proj(3)

    pos = start_pos + t * tm + lax.broadcasted_iota(jnp.int32, (tm, 1), 0)
    for g, w in enumerate(POOL_WINDOWS):
        cols = slice(g * GC, (g + 1) * GC)
        win = u[:, cols]
        for dlt in range(1, w):
            win = win + pbuf[HALO_P - dlt:HALO_P - dlt + tm, cols]
        cnt = jnp.minimum(pos + 1, w).astype(F32)
        mdiff = (win / cnt - u[:, cols]).astype(BF16)
        mix_ref[:, cols] = (_dot(mdiff, wpool_ref[g]) * pscale_ref[:, cols]).astype(BF16)

    first = HALO_D - (CONV_D - 1)
    conv = cw_ref[0:1, :] * zbuf[first:first + tm, :]
    for j in range(1, CONV_D):
        conv = conv + cw_ref[j:j + 1, :] * zbuf[first + j:first + j + tm, :]
    mix_ref[:, D_C:D_C + D_D] = (bg * conv).astype(BF16)

    r = _dot(mix_ref[...], wout_ref[...])
    o_ref[...] = x + (1.0 + _mod(gt_ref)) * _rms(r, gpost_ref[...])

    ptail = pbuf[tm:tm + HALO_P, :]
    pst_ref[0] = ptail
    pbuf[0:HALO_P, :] = ptail
    ztail = zbuf[tm:tm + HALO_D, :]
    dst_ref[0] = ztail
    zbuf[0:HALO_D, :] = ztail


def _odd_call(x, sh, sc, gt, gpre, gpost, w_in, w_out, w_pool, pscale, cw, *, nb, nt, tm, start_pos):
    n, d = x.shape
    row = pl.BlockSpec((tm, d), lambda b, t: (b * nt + t, 0))
    out_shape = (jax.ShapeDtypeStruct((n, d), F32),
                 jax.ShapeDtypeStruct((nb, HALO_P, D_C), F32),
                 jax.ShapeDtypeStruct((nb, HALO_D, D_D), F32))
    return pl.pallas_call(
        functools.partial(_odd_body, tm=tm, start_pos=start_pos),
        out_shape=out_shape,
        grid=(nb, nt),
        in_specs=[row, _mod_spec(sh, tm), _mod_spec(sc, tm), _mod_spec(gt, tm),
                  _resident((1, d)), _resident((1, d)), _resident(w_in.shape), _resident(w_out.shape),
                  _resident(w_pool.shape), _resident((1, D_C)), _resident(cw.shape)],
        out_specs=(row,
                   pl.BlockSpec((1, HALO_P, D_C), lambda b, t: (b, 0, 0)),
                   pl.BlockSpec((1, HALO_D, D_D), lambda b, t: (b, 0, 0))),
        scratch_shapes=[pltpu.VMEM((HALO_P + tm, D_C), F32),
                        pltpu.VMEM((HALO_D + tm, D_D), F32),
                        pltpu.VMEM((tm, D_C + D_D), BF16)],
        compiler_params=_params(("arbitrary", "arbitrary")),
        name="odd_mixer_prompt",
    )(x, sh, sc, gt, gpre, gpost, w_in, w_out, w_pool, pscale, cw)


def _even_front_sample_body(x_ref, sh_ref, sc_ref, gpre_ref, win_ref, hist_ref, cw_ref, cb_ref,
                            lng_ref, lnb_ref, a_ref, q_ref, k_ref, v_ref, u_ref):
    x = x_ref[...]
    h = _pre(x, gpre_ref[...], sc_ref[...], sh_ref[...]).astype(BF16)

    def proj(c):
        return _dot(h, win_ref[:, c * D_A:(c + 1) * D_A])

    u = proj(0) * jax.nn.sigmoid(proj(1))
    u_ref[...] = u
    q_ref[...] = proj(2) * (DH_B ** -0.5)
    k_ref[...] = proj(3)
    v_ref[...] = proj(4)

    acc = cb_ref[...] + cw_ref[CONV_A - 1:CONV_A, :] * u
    for j in range(CONV_A - 1):
        acc = acc + cw_ref[j:j + 1, :] * hist_ref[j]
    mu = jnp.mean(acc, axis=-1, keepdims=True)
    cen = acc - mu
    var = jnp.mean(cen * cen, axis=-1, keepdims=True)
    y = cen * lax.rsqrt(var + EPS) * lng_ref[...] + lnb_ref[...]
    a_ref[...] = _silu(y)


def _even_front_sample_call(x, sh, sc, gpre, w_in, hist_t, cw, cb, lng, lnb):
    n, d = x.shape
    full = lambda shape: pl.BlockSpec(shape, lambda i: (0,) * len(shape))
    out = jax.ShapeDtypeStruct((n, D_A), F32)
    return pl.pallas_call(
        _even_front_sample_body,
        out_shape=(out, out, out, out, out),
        grid=(1,),
        in_specs=[full((n, d)), full((n, d)), full((n, d)), full((1, d)), full(w_in.shape),
                  full(hist_t.shape), full(cw.shape), full((1, D_A)), full((1, D_A)), full((1, D_A))],
        out_specs=tuple(full((n, D_A)) for _ in range(5)),
        compiler_params=_params(("arbitrary",)),
        name="even_front_sample",
    )(x, sh, sc, gpre, w_in, hist_t, cw, cb, lng, lnb)


def _decode_attn_body(pt_ref, q_ref, kn_ref, vn_ref, lamqk_ref, g_ref, ck_hbm, cv_hbm, o_ref,
                      kbuf, vbuf, sem, *, lam_init, n_pages, pc):
    b = pl.program_id(0)
    nb = pl.num_programs(0)
    nc = n_pages // pc
    rows = pc * PAGE_SIZE
    n_comp = 2 * H_B

    def chunk_copies(seq, c, slot):
        copies = []
        for i in range(pc):
            page = pt_ref[seq * n_pages + c * pc + i]
            copies.append(pltpu.make_async_copy(ck_hbm.at[page], kbuf.at[slot, i], sem.at[0, slot]))
            copies.append(pltpu.make_async_copy(cv_hbm.at[page], vbuf.at[slot, i], sem.at[1, slot]))
        return copies

    def start_chunk(seq, c, slot):
        for cp in chunk_copies(seq, c, slot):
            cp.start()

    def wait_chunk(seq, c, slot):
        for cp in chunk_copies(seq, c, slot):
            cp.wait()

    @pl.when(b == 0)
    def _():
        start_chunk(0, 0, 0)

    lane = lax.broadcasted_iota(jnp.int32, (n_comp, D_QK), 1)
    comp = lax.broadcasted_iota(jnp.int32, (n_comp, D_QK), 0)
    q_row = q_ref[pl.ds(b, 1), :]
    q8 = jnp.where(lane // DH_B == comp, jnp.broadcast_to(q_row, (n_comp, D_QK)), 0.0)
    q8b = q8.astype(BF16)

    def body(c, carry):
        m, l, acc = carry
        slot = lax.rem(c, 2)
        wait_chunk(b, c, slot)

        @pl.when(c + 1 < nc)
        def _():
            start_chunk(b, c + 1, 1 - slot)

        @pl.when(jnp.logical_and(c + 1 == nc, b + 1 < nb))
        def _():
            start_chunk(b + 1, 0, 1 - slot)

        kc = kbuf[slot].reshape(rows, D_QK).astype(BF16)
        s = lax.dot_general(q8b, kc, (((1,), (1,)), ((), ())), preferred_element_type=F32)
        m_new = jnp.maximum(m, jnp.max(s, axis=-1, keepdims=True))
        alpha = jnp.exp(m - m_new)
        p = jnp.exp(s - m_new)
        l = alpha * l + jnp.sum(p, axis=-1, keepdims=True)
        vc = vbuf[slot].reshape(rows, D_BO).astype(BF16)
        acc = alpha * acc + _dot(p.astype(BF16), vc)
        return m_new, l, acc

    init = (jnp.full((n_comp, 1), -jnp.inf, F32), jnp.zeros((n_comp, 1), F32),
            jnp.zeros((n_comp, D_BO), F32))
    m, l, acc = lax.fori_loop(0, nc, body, init)

    s_new = jnp.sum(q8 * kn_ref[pl.ds(b, 1), :], axis=-1, keepdims=True)
    m_fin = jnp.maximum(m, s_new)
    alpha = jnp.exp(m - m_fin)
    p_new = jnp.exp(s_new - m_fin)
    l = alpha * l + p_new
    acc = alpha * acc + p_new * vn_ref[pl.ds(b, 1), :]

    lam = _diff_lambda(lamqk_ref, lam_init)
    head = lane // DV_B
    coef = jnp.where(comp == 2 * head, 1.0, jnp.where(comp == 2 * head + 1, -lam, 0.0))
    o = jnp.sum(coef * (acc / l), axis=0, keepdims=True)
    o2 = o * o
    lane1 = lane[0:1]
    inv = jnp.zeros_like(o)
    for hh in range(H_B):
        in_head = lane1 // DV_B == hh
        ms = jnp.sum(jnp.where(in_head, o2, 0.0), axis=-1, keepdims=True) / DV_B
        inv = jnp.where(in_head, lax.rsqrt(ms + EPS), inv)
    o_ref[pl.ds(b, 1), :] = o * inv * g_ref[...] * (1.0 - lam_init)


def _decode_attn_call(page_table, q, k_new, v_new, lam_qk, g_tiled, cache_k, cache_v, *, lam_init):
    nb, n_pages = page_table.shape
    pc = PAGES_PER_CHUNK
    full = lambda shape: pl.BlockSpec(shape, lambda i, pt: (0,) * len(shape))
    grid_spec = pltpu.PrefetchScalarGridSpec(
        num_scalar_prefetch=1,
        grid=(nb,),
        in_specs=[full(q.shape), full(k_new.shape), full(v_new.shape), full(lam_qk.shape),
                  full(g_tiled.shape),
                  pl.BlockSpec(memory_space=pl.ANY), pl.BlockSpec(memory_space=pl.ANY)],
        out_specs=full((nb, D_BO)),
        scratch_shapes=[pltpu.VMEM((2, pc, PAGE_SIZE, D_QK), F32),
                        pltpu.VMEM((2, pc, PAGE_SIZE, D_BO), F32),
                        pltpu.SemaphoreType.DMA((2, 2))])
    return pl.pallas_call(
        functools.partial(_decode_attn_body, lam_init=lam_init, n_pages=n_pages, pc=pc),
        out_shape=jax.ShapeDtypeStruct((nb, D_BO), F32),
        grid_spec=grid_spec,
        compiler_params=_params(("arbitrary",)),
        name="diff_attn_decode",
    )(page_table.reshape(-1), q, k_new, v_new, lam_qk, g_tiled, cache_k, cache_v)


def _odd_sample_body(x_ref, sh_ref, sc_ref, gt_ref, gpre_ref, gpost_ref, win_ref, wout_ref, wpool_ref,
                     pscale_ref, cw_ref, phist_ref, dhist_ref, o_ref, u_ref, z_ref, mix_ref, *, start_pos):
    x = x_ref[...]
    h = _pre(x, gpre_ref[...], sc_ref[...], sh_ref[...]).astype(BF16)

    def proj(c):
        return _dot(h, win_ref[:, c * D_C:(c + 1) * D_C])

    u = proj(0)
    u_ref[...] = u
    bg = proj(1)
    z = proj(2) * proj(3)
    z_ref[...] = z

    for g, w in enumerate(POOL_WINDOWS):
        cols = slice(g * GC, (g + 1) * GC)
        win = u[:, cols]
        for dlt in range(1, w):
            win = win + phist_ref[POOL_HIST - dlt][:, cols]
        cnt = float(min(start_pos + 1, w))
        mdiff = (win / cnt - u[:, cols]).astype(BF16)
        mix_ref[:, cols] = (_dot(mdiff, wpool_ref[g]) * pscale_ref[:, cols]).astype(BF16)

    conv = cw_ref[CONV_D - 1:CONV_D, :] * z
    for j in range(CONV_D - 1):
        conv = conv + cw_ref[j:j + 1, :] * dhist_ref[j]
    mix_ref[:, D_C:D_C + D_D] = (bg * conv).astype(BF16)

    r = _dot(mix_ref[...], wout_ref[...])
    o_ref[...] = x + (1.0 + gt_ref[...]) * _rms(r, gpost_ref[...])


def _odd_sample_call(x, sh, sc, gt, gpre, gpost, w_in, w_out, w_pool, pscale, cw, phist_t, dhist_t,
                     *, start_pos):
    n, d = x.shape
    full = lambda shape: pl.BlockSpec(shape, lambda i: (0,) * len(shape))
    return pl.pallas_call(
        functools.partial(_odd_sample_body, start_pos=start_pos),
        out_shape=(jax.ShapeDtypeStruct((n, d), F32),
                   jax.ShapeDtypeStruct((n, D_C), F32),
                   jax.ShapeDtypeStruct((n, D_D), F32)),
        grid=(1,),
        in_specs=[full((n, d)), full((n, d)), full((n, d)), full((n, d)), full((1, d)), full((1, d)),
                  full(w_in.shape), full(w_out.shape), full(w_pool.shape), full((1, D_C)),
                  full(cw.shape), full(phist_t.shape), full(dhist_t.shape)],
        out_specs=(full((n, d)), full((n, D_C)), full((n, D_D))),
        scratch_shapes=[pltpu.VMEM((n, D_C + D_D), BF16)],
        compiler_params=_params(("arbitrary",)),
        name="odd_mixer_sample",
    )(x, sh, sc, gt, gpre, gpost, w_in, w_out, w_pool, pscale, cw, phist_t, dhist_t)


def kernel(x_prompt, x_sample, cache_k, cache_v, state_conv_a, state_pool, state_conv_d, page_table,
           c_prompt, c_sample, w_ada, b_ada, norm_g, w_ffn_in, w_ffn_out, w_in_even, w_out_even,
           conv_a_w, conv_a_b, ln_a_g, ln_a_b, lam_qk, subln_g, w_in_odd, w_out_odd, w_pool,
           pool_scale, conv_d_w):
    bp, seq, d = x_prompt.shape
    bs = x_sample.shape[0]
    assert x_sample.shape[1] == 1 and d == D_MODEL and seq % TOKEN_TILE == 0
    past_len = page_table.shape[1] * PAGE_SIZE
    nt = seq // TOKEN_TILE

    n_c = bp + bs
    pad = (-n_c) % SUBLANES
    c_all = jnp.concatenate([c_prompt, c_sample, jnp.zeros((pad, d), F32)], axis=0)
    mod = _ada_call(c_all, w_ada, b_ada)

    def mods(layer, sub):
        out_p, out_s = [], []
        for j in range(3):
            col = (sub * 3 + j) * d
            out_p.append(mod[layer, :bp, col:col + d].reshape(bp, 1, d))
            out_s.append(mod[layer, bp:n_c, col:col + d])
        return out_p, out_s

    w_ffn_in_b = w_ffn_in.astype(BF16)
    w_ffn_out_b = w_ffn_out.astype(BF16)
    w_in_even_b = w_in_even.astype(BF16)
    w_out_even_b = w_out_even.astype(BF16)
    w_in_odd_b = w_in_odd.astype(BF16)
    w_out_odd_b = w_out_odd.astype(BF16)
    w_pool_b = w_pool.astype(BF16)

    xp = x_prompt.reshape(bp * seq, d)
    xs = x_sample.reshape(bs, d)
    geo_p = dict(nb=bp, nt=nt, tm=TOKEN_TILE)
    geo_s = dict(nb=1, nt=1, tm=bs)

    def ffn(x, m, layer, sub, which, geo, name):
        sh, sc, gt = m
        return _ffn_call(x, sh, sc, gt, norm_g[layer, sub, 0][None], norm_g[layer, sub, 1][None],
                         w_ffn_in_b[layer, which], w_ffn_out_b[layer, which], resw=0.5, name=name, **geo)

    outs = {}
    for layer in range(DEPTH):
        (mp0, ms0), (mp1, ms1), (mp2, ms2) = mods(layer, 0), mods(layer, 1), mods(layer, 2)
        xp = ffn(xp, mp0, layer, 0, 0, geo_p, "ffn_prompt")
        xs = ffn(xs, ms0, layer, 0, 0, geo_s, "ffn_sample")
        gpre, gpost = norm_g[layer, 1, 0][None], norm_g[layer, 1, 1][None]
        if layer % 2 == 0:
            e = layer // 2
            lam_init = _lambda_init(layer)
            cw, cb = conv_a_w[e], conv_a_b[e][None]
            lng, lnb = ln_a_g[e][None], ln_a_b[e][None]
            a, q, kb, vb, k, v, st = _even_front_call(xp, mp1[0], mp1[1], gpre, w_in_even_b[e], cw, cb,
                                                      lng, lnb, **geo_p)
            o = _attn_call(q, kb, vb, lam_qk[e], subln_g[e][None], nb=bp, seq=seq, lam_init=lam_init)
            xp = _mixer_back_call(a, o, xp, mp1[2], gpost, w_out_even_b[e], name="even_back_prompt", **geo_p)
            outs["k_prompt"] = k.reshape(bp, seq, H_B, 2 * DH_B)
            outs["v_prompt"] = v.reshape(bp, seq, H_B, DV_B)
            outs["conv_a_prompt"] = st[:, HALO_A - (CONV_A - 1):, :]
            hist = state_conv_a[e]
            a_s, q_s, k_s, v_s, u_s = _even_front_sample_call(
                xs, ms1[0], ms1[1], gpre, w_in_even_b[e], jnp.swapaxes(hist, 0, 1), cw, cb, lng, lnb)
            o_s = _decode_attn_call(page_table, q_s, k_s, v_s, lam_qk[e], jnp.tile(subln_g[e], H_B)[None],
                                    cache_k[e].reshape(-1, PAGE_SIZE, D_QK),
                                    cache_v[e].reshape(-1, PAGE_SIZE, D_BO), lam_init=lam_init)
            xs = _mixer_back_call(a_s, o_s, xs, ms1[2], gpost, w_out_even_b[e], name="even_back_sample", **geo_s)
            outs["k_sample"] = k_s.reshape(bs, 1, H_B, 2 * DH_B)
            outs["v_sample"] = v_s.reshape(bs, 1, H_B, DV_B)
            outs["conv_a_sample"] = jnp.concatenate([hist[:, 1:], u_s[:, None, :]], axis=1)
        else:
            od = layer // 2
            xp, pst, dst = _odd_call(xp, mp1[0], mp1[1], mp1[2], gpre, gpost, w_in_odd_b[od], w_out_odd_b[od],
                                     w_pool_b[od], pool_scale[od][None], conv_d_w[od], start_pos=0, **geo_p)
            outs["pool_prompt"] = pst[:, HALO_P - POOL_HIST:, :]
            outs["conv_d_prompt"] = dst[:, HALO_D - (CONV_D - 1):, :]
            phist, dhist = state_pool[od], state_conv_d[od]
            xs, u_s, z_s = _odd_sample_call(xs, ms1[0], ms1[1], ms1[2], gpre, gpost, w_in_odd_b[od],
                                            w_out_odd_b[od], w_pool_b[od], pool_scale[od][None], conv_d_w[od],
                                            jnp.swapaxes(phist, 0, 1), jnp.swapaxes(dhist, 0, 1),
                                            start_pos=past_len)
            outs["pool_sample"] = jnp.concatenate([phist[:, 1:], u_s[:, None, :]], axis=1)
            outs["conv_d_sample"] = jnp.concatenate([dhist[:, 1:], z_s[:, None, :]], axis=1)
        xp = ffn(xp, mp2, layer, 2, 1, geo_p, "ffn_prompt")
        xs = ffn(xs, ms2, layer, 2, 1, geo_s, "ffn_sample")

    stack = lambda name: outs[name][None]
    return (xp.reshape(bp, seq, d), xs.reshape(bs, 1, d),
            stack("k_prompt"), stack("v_prompt"), stack("k_sample"), stack("v_sample"),
            stack("conv_a_prompt"), stack("conv_a_sample"),
            stack("pool_prompt"), stack("pool_sample"),
            stack("conv_d_prompt"), stack("conv_d_sample"))
```

```python
import functools
import math

import jax
import jax.numpy as jnp
from jax import lax
from jax.experimental import pallas as pl
from jax.experimental.pallas import tpu as pltpu

F32 = jnp.float32
BF16 = jnp.bfloat16

D_MODEL = 1024
DEPTH = 2
PAGE_SIZE = 128
D_A = D_MODEL // 2
CONV_A = 31
DH_B = 64
DV_B = 2 * DH_B
H_B = (D_MODEL // 2) // DV_B
D_QK = H_B * 2 * DH_B
D_BO = H_B * DV_B
D_C = D_MODEL // 2
POOL_WINDOWS = (2, 4, 8, 16)
GC = D_C // len(POOL_WINDOWS)
POOL_HIST = max(POOL_WINDOWS) - 1
D_D = D_MODEL // 2
CONV_D = 3
D_FF = ((8 * D_MODEL // 3 + 255) // 256) * 256
N_SUB = 3
EPS = 1e-6

SUBLANES = 8
LANES = 128
VMEM_LIMIT_BYTES = 56 * 1024 * 1024

TOKEN_TILE = 512
FF_CHUNK = 256
CONV_ROWS = 128
HALO_A = 32
HALO_P = 16
HALO_D = 8
QK_SCALE_LOG2 = (DH_B ** -0.5) * math.log2(math.e)
ATT_TQ = 256
ATT_TK = 512
PAGES_PER_CHUNK = 8
DECODE_SLOTS = 4


def _lambda_init(layer):
    return 0.8 - 0.6 * math.exp(-0.3 * layer)


def _params(semantics):
    return pltpu.CompilerParams(dimension_semantics=semantics, vmem_limit_bytes=VMEM_LIMIT_BYTES)


def _rms(x, g):
    return x * lax.rsqrt(jnp.mean(x * x, axis=-1, keepdims=True) + EPS) * g


def _mod(ref):
    v = ref[...]
    return v[0] if v.ndim == 3 else v


def _pre(x, g, scale, shift):
    return _rms(x, g) * (1.0 + scale) + shift


def _silu(x):
    return x * jax.nn.sigmoid(x)


def _dot(a, b):
    return jnp.dot(a, b, preferred_element_type=F32)


def _resident(shape):
    return pl.BlockSpec(shape, lambda *_: (0,) * len(shape), pipeline_mode=pl.Buffered(1))


def _mod_spec(arr, tm):
    if arr.ndim == 3:
        return pl.BlockSpec((1, 1, arr.shape[-1]), lambda b, t: (b, 0, 0))
    return pl.BlockSpec((tm, arr.shape[-1]), lambda b, t: (0, 0))


def _ada_body(c_ref, w_ref, b_ref, o_ref):
    c = c_ref[...]
    o_ref[0] = _dot(_silu(c).astype(BF16), w_ref[0].astype(BF16)) + b_ref[0]


def _ada_call(c_all, w_ada, b_ada):
    rows, d = c_all.shape
    depth, _, n = w_ada.shape
    tn = 1024
    return pl.pallas_call(
        _ada_body,
        out_shape=jax.ShapeDtypeStruct((depth, rows, n), F32),
        grid=(depth, n // tn),
        in_specs=[pl.BlockSpec((rows, d), lambda i, j: (0, 0)),
                  pl.BlockSpec((1, d, tn), lambda i, j: (i, 0, j)),
                  pl.BlockSpec((1, 1, tn), lambda i, j: (i, 0, j))],
        out_specs=pl.BlockSpec((1, rows, tn), lambda i, j: (i, 0, j)),
        compiler_params=_params(("parallel", "parallel")),
        name="ada_mod",
    )(c_all, w_ada, b_ada.reshape(depth, 1, n))


def _ffn_body(x_ref, sh_ref, sc_ref, gt_ref, gpre_ref, gpost_ref, win_ref, wout_ref, o_ref, act_ref,
              *, resw):
    x = x_ref[...]
    h = _pre(x, gpre_ref[...], _mod(sc_ref), _mod(sh_ref)).astype(BF16)
    for j in range(D_FF // FF_CHUNK):
        lo = j * FF_CHUNK
        g = _dot(h, win_ref[:, lo:lo + FF_CHUNK])
        u = _dot(h, win_ref[:, D_FF + lo:D_FF + lo + FF_CHUNK])
        act_ref[:, lo:lo + FF_CHUNK] = (_silu(g) * u).astype(BF16)
    o = _dot(act_ref[...], wout_ref[...])
    o_ref[...] = x + resw * (1.0 + _mod(gt_ref)) * _rms(o, gpost_ref[...])


def _ffn_call(x, sh, sc, gt, gpre, gpost, w_in, w_out, *, resw, nb, nt, tm, name):
    n, d = x.shape
    row = pl.BlockSpec((tm, d), lambda b, t: (b * nt + t, 0))
    return pl.pallas_call(
        functools.partial(_ffn_body, resw=resw),
        out_shape=jax.ShapeDtypeStruct((n, d), F32),
        grid=(nb, nt),
        in_specs=[row, _mod_spec(sh, tm), _mod_spec(sc, tm), _mod_spec(gt, tm),
                  _resident((1, d)), _resident((1, d)),
                  _resident(w_in.shape), _resident(w_out.shape)],
        out_specs=row,
        scratch_shapes=[pltpu.VMEM((tm, D_FF), BF16)],
        compiler_params=_params(("parallel", "parallel")),
        name=name,
    )(x, sh, sc, gt, gpre, gpost, w_in, w_out)


def _even_front_body(x_ref, sh_ref, sc_ref, gpre_ref, win_ref, cw_ref, cb_ref, lng_ref, lnb_ref,
                     a_ref, qt_ref, kb_ref, vt_ref, k_ref, v_ref, st_ref, ubuf, *, tm):
    @pl.when(pl.program_id(1) == 0)
    def _():
        ubuf[0:HALO_A, :] = jnp.zeros((HALO_A, D_A), F32)

    x = x_ref[...]
    h = _pre(x, gpre_ref[...], _mod(sc_ref), _mod(sh_ref)).astype(BF16)

    def proj(c):
        return _dot(h, win_ref[:, c * D_A:(c + 1) * D_A])

    ubuf[HALO_A:HALO_A + tm, :] = proj(0) * jax.nn.sigmoid(proj(1))
    qt_ref[...] = (proj(2) * QK_SCALE_LOG2).T.astype(BF16)
    k = proj(3)
    k_ref[...] = k
    kb_ref[...] = k.astype(BF16)
    v = proj(4)
    v_ref[...] = v
    vt_ref[...] = v.T.astype(BF16)

    first = HALO_A - (CONV_A - 1)
    for c in range(tm // CONV_ROWS):
        base = c * CONV_ROWS
        acc = jnp.broadcast_to(cb_ref[...], (CONV_ROWS, D_A))
        for res in range(SUBLANES):
            span = CONV_ROWS + (SUBLANES if res else 0)
            part = None
            for blk in range((first + CONV_A - 1) // SUBLANES + 1):
                j = blk * SUBLANES + res - first
                if 0 <= j < CONV_A:
                    lo = base + blk * SUBLANES
                    term = cw_ref[j:j + 1, :] * ubuf[lo:lo + span, :]
                    part = term if part is None else part + term
            acc = acc + part[res:res + CONV_ROWS, :]
        mu = jnp.mean(acc, axis=-1, keepdims=True)
        cen = acc - mu
        var = jnp.mean(cen * cen, axis=-1, keepdims=True)
        y = cen * lax.rsqrt(var + EPS) * lng_ref[...] + lnb_ref[...]
        a_ref[base:base + CONV_ROWS, :] = _silu(y).astype(BF16)

    tail = ubuf[tm:tm + HALO_A, :]
    st_ref[0] = tail
    ubuf[0:HALO_A, :] = tail


def _even_front_call(x, sh, sc, gpre, w_in, cw, cb, lng, lnb, *, nb, nt, tm):
    n, d = x.shape
    row = lambda w: pl.BlockSpec((tm, w), lambda b, t: (b * nt + t, 0))
    out_shape = (jax.ShapeDtypeStruct((n, D_A), BF16),
                 jax.ShapeDtypeStruct((D_QK, n), BF16),
                 jax.ShapeDtypeStruct((n, D_QK), BF16),
                 jax.ShapeDtypeStruct((D_BO, n), BF16),
                 jax.ShapeDtypeStruct((n, D_QK), F32),
                 jax.ShapeDtypeStruct((n, D_BO), F32),
                 jax.ShapeDtypeStruct((nb, HALO_A, D_A), F32))
    return pl.pallas_call(
        functools.partial(_even_front_body, tm=tm),
        out_shape=out_shape,
        grid=(nb, nt),
        in_specs=[row(d), _mod_spec(sh, tm), _mod_spec(sc, tm), _resident((1, d)),
                  _resident(w_in.shape), _resident(cw.shape), _resident((1, D_A)),
                  _resident((1, D_A)), _resident((1, D_A))],
        out_specs=(row(D_A),
                   pl.BlockSpec((D_QK, tm), lambda b, t: (0, b * nt + t)),
                   row(D_QK),
                   pl.BlockSpec((D_BO, tm), lambda b, t: (0, b * nt + t)),
                   row(D_QK), row(D_BO),
                   pl.BlockSpec((1, HALO_A, D_A), lambda b, t: (b, 0, 0))),
        scratch_shapes=[pltpu.VMEM((HALO_A + tm, D_A), F32)],
        compiler_params=_params(("arbitrary", "arbitrary")),
        name="even_front_prompt",
    )(x, sh, sc, gpre, w_in, cw, cb, lng, lnb)


def _diff_lambda(lamqk_ref, lam_init):
    lq = lamqk_ref[...]
    s1 = jnp.sum(lq[0:1] * lq[1:2], axis=1, keepdims=True)
    s2 = jnp.sum(lq[2:3] * lq[3:4], axis=1, keepdims=True)
    return jnp.exp(s1) - jnp.exp(s2) + lam_init


def _attn_body(q_ref, k_ref, vt_ref, lamqk_ref, g_ref, o_ref, qt_ref, sa_ref, sb_ref, m_ref, l_ref, acc_ref,
               *, lam_init, tq, tk):
    qi = pl.program_id(2)
    q = q_ref[...]
    dim = lax.broadcasted_iota(jnp.int32, q.shape, 0)
    zero = jnp.zeros_like(q)
    qt_ref[:, 0:tq] = jnp.where(dim < DH_B, q, zero)
    qt_ref[:, tq:2 * tq] = jnp.where(dim >= DH_B, q, zero)

    m_ref[...] = jnp.full(m_ref.shape, -jnp.inf, F32)
    l_ref[...] = jnp.zeros(l_ref.shape, F32)
    acc_ref[...] = jnp.zeros(acc_ref.shape, F32)

    def key_start(kt):
        return pl.multiple_of(kt * tk, tk)

    def scores(kt):
        return _dot(k_ref[pl.ds(key_start(kt), tk), :], qt_ref[...])

    def consume(s_ref, kt, masked):
        start = key_start(kt)
        for half in range(2):
            cols = slice(half * tq, (half + 1) * tq)
            s = s_ref[:, cols]
            if masked:
                kpos = start + lax.broadcasted_iota(jnp.int32, s.shape, 0)
                qpos = qi * tq + lax.broadcasted_iota(jnp.int32, s.shape, 1)
                s = jnp.where(kpos <= qpos, s, -jnp.inf)
            m_old = m_ref[:, cols]
            m_new = jnp.maximum(m_old, jnp.max(s, axis=0, keepdims=True))
            alpha = jnp.exp2(m_old - m_new)
            p = jnp.exp2(s - m_new)
            l_ref[:, cols] = alpha * l_ref[:, cols] + jnp.sum(p, axis=0, keepdims=True)
            pv = _dot(vt_ref[:, pl.ds(start, tk)], p.astype(BF16))
            acc_ref[:, cols] = alpha * acc_ref[:, cols] + pv
            m_ref[:, cols] = m_new

    n_full = (qi * tq) // tk
    sa_ref[...] = scores(0)

    def pair(j, carry):
        sb_ref[...] = scores(2 * j + 1)
        consume(sa_ref, 2 * j, masked=False)
        sa_ref[...] = scores(2 * j + 2)
        consume(sb_ref, 2 * j + 1, masked=False)
        return carry

    lax.fori_loop(0, n_full // 2, pair, 0)

    @pl.when(n_full % 2 == 1)
    def _():
        sb_ref[...] = scores(n_full)
        consume(sa_ref, n_full - 1, masked=False)
        consume(sb_ref, n_full, masked=True)

    @pl.when(n_full % 2 == 0)
    def _():
        consume(sa_ref, n_full, masked=True)

    lam = _diff_lambda(lamqk_ref, lam_init)
    nt = acc_ref[...] / l_ref[...]
    ot = nt[:, 0:tq] - lam * nt[:, tq:2 * tq]
    ot = ot * lax.rsqrt(jnp.mean(ot * ot, axis=0, keepdims=True) + EPS)
    o_ref[...] = (ot.T * g_ref[...] * (1.0 - lam_init)).astype(o_ref.dtype)


def _attn_call(qt, k, vt, lam_qk, g, *, nb, seq, lam_init):
    n = k.shape[0]
    nq = seq // ATT_TQ
    return pl.pallas_call(
        functools.partial(_attn_body, lam_init=lam_init, tq=ATT_TQ, tk=ATT_TK),
        out_shape=jax.ShapeDtypeStruct((n, D_BO), BF16),
        grid=(nb, H_B, nq),
        in_specs=[pl.BlockSpec((DV_B, ATT_TQ), lambda b, h, i: (h, b * nq + i)),
                  pl.BlockSpec((seq, DV_B), lambda b, h, i: (b, h)),
                  pl.BlockSpec((DV_B, seq), lambda b, h, i: (h, b)),
                  pl.BlockSpec(lam_qk.shape, lambda b, h, i: (0, 0)),
                  pl.BlockSpec((1, DV_B), lambda b, h, i: (0, 0))],
        out_specs=pl.BlockSpec((ATT_TQ, DV_B), lambda b, h, i: (b * nq + i, h)),
        scratch_shapes=[pltpu.VMEM((DV_B, 2 * ATT_TQ), BF16),
                        pltpu.VMEM((ATT_TK, 2 * ATT_TQ), F32),
                        pltpu.VMEM((ATT_TK, 2 * ATT_TQ), F32),
                        pltpu.VMEM((1, 2 * ATT_TQ), F32),
                        pltpu.VMEM((1, 2 * ATT_TQ), F32),
                        pltpu.VMEM((DV_B, 2 * ATT_TQ), F32)],
        compiler_params=_params(("parallel", "parallel", "parallel")),
        name="diff_attn_prompt",
    )(qt, k, vt, lam_qk, g)


def _mixer_back_body(a_ref, b_ref, x_ref, gt_ref, gpost_ref, w_ref, o_ref):
    half = a_ref.shape[-1]
    r = (_dot(a_ref[...].astype(BF16), w_ref[0:half, :])
         + _dot(b_ref[...].astype(BF16), w_ref[half:2 * half, :]))
    o_ref[...] = x_ref[...] + (1.0 + _mod(gt_ref)) * _rms(r, gpost_ref[...])


def _mixer_back_call(a, b, x, gt, gpost, w_out, *, nb, nt, tm, name):
    n, d = x.shape
    row = lambda w: pl.BlockSpec((tm, w), lambda bb, t: (bb * nt + t, 0))
    return pl.pallas_call(
        _mixer_back_body,
        out_shape=jax.ShapeDtypeStruct((n, d), F32),
        grid=(nb, nt),
        in_specs=[row(a.shape[-1]), row(b.shape[-1]), row(d), _mod_spec(gt, tm),
                  _resident((1, d)), _resident(w_out.shape)],
        out_specs=row(d),
        compiler_params=_params(("parallel", "parallel")),
        name=name,
    )(a, b, x, gt, gpost, w_out)


def _odd_body(x_ref, sh_ref, sc_ref, gt_ref, gpre_ref, gpost_ref, win_ref, wout_ref, wpool_ref,
              pscale_ref, cw_ref, o_ref, pst_ref, dst_ref, pbuf, zbuf, mix_ref, *, tm, start_pos):
    t = pl.program_id(1)

    @pl.when(t == 0)
    def _():
        pbuf[0:HALO_P, :] = jnp.zeros((HALO_P, D_C), F32)
        zbuf[0:HALO_D, :] = jnp.zeros((HALO_D, D_D), F32)

    x = x_ref[...]
    h = _pre(x, gpre_ref[...], _mod(sc_ref), _mod(sh_ref)).astype(BF16)

    def proj(c):
        return _dot(h, win_ref[:, c * D_C:(c + 1) * D_C])

    u = proj(0)
    pbuf[HALO_P:HALO_P + tm, :] = u
    bg = proj(1)
    zbuf[HALO_D:HALO_D + tm, :] = proj(2) * proj(3)

    pos = start_pos + t * tm + lax.broadcasted_iota(jnp.int32, (tm, 1), 0)
    for g, w in enumerate(POOL_WINDOWS):
        cols = slice(g * GC, (g + 1) * GC)
        win = u[:, cols]
        for dlt in range(1, w):
            win = win + pbuf[HALO_P - dlt:HALO_P - dlt + tm, cols]
        cnt = jnp.minimum(pos + 1, w).astype(F32)
        mdiff = (win / cnt - u[:, cols]).astype(BF16)
        mix_ref[:, cols] = (_dot(mdiff, wpool_ref[g]) * pscale_ref[:, cols]).astype(BF16)

    first = HALO_D - (CONV_D - 1)
    conv = cw_ref[0:1, :] * zbuf[first:first + tm, :]
    for j in range(1, CONV_D):
        conv = conv + cw_ref[j:j + 1, :] * zbuf[first + j:first + j + tm, :]
    mix_ref[:, D_C:D_C + D_D] = (bg * conv).astype(BF16)

    r = _dot(mix_ref[...], wout_ref[...])
    o_ref[...] = x + (1.0 + _mod(gt_ref)) * _rms(r, gpost_ref[...])

    ptail = pbuf[tm:tm + HALO_P, :]
    pst_ref[0] = ptail
    pbuf[0:HALO_P, :] = ptail
    ztail = zbuf[tm:tm + HALO_D, :]
    dst_ref[0] = ztail
    zbuf[0:HALO_D, :] = ztail


def _odd_call(x, sh, sc, gt, gpre, gpost, w_in, w_out, w_pool, pscale, cw, *, nb, nt, tm, start_pos):
    n, d = x.shape
    row = pl.BlockSpec((tm, d), lambda b, t: (b * nt + t, 0))
    out_shape = (jax.ShapeDtypeStruct((n, d), F32),
                 jax.ShapeDtypeStruct((nb, HALO_P, D_C), F32),
                 jax.ShapeDtypeStruct((nb, HALO_D, D_D), F32))
    return pl.pallas_call(
        functools.partial(_odd_body, tm=tm, start_pos=start_pos),
        out_shape=out_shape,
        grid=(nb, nt),
        in_specs=[row, _mod_spec(sh, tm), _mod_spec(sc, tm), _mod_spec(gt, tm),
                  _resident((1, d)), _resident((1, d)), _resident(w_in.shape), _resident(w_out.shape),
                  _resident(w_pool.shape), _resident((1, D_C)), _resident(cw.shape)],
        out_specs=(row,
                   pl.BlockSpec((1, HALO_P, D_C), lambda b, t: (b, 0, 0)),
                   pl.BlockSpec((1, HALO_D, D_D), lambda b, t: (b, 0, 0))),
        scratch_shapes=[pltpu.VMEM((HALO_P + tm, D_C), F32),
                        pltpu.VMEM((HALO_D + tm, D_D), F32),
                        pltpu.VMEM((tm, D_C + D_D), BF16)],
        compiler_params=_params(("arbitrary", "arbitrary")),
        name="odd_mixer_prompt",
    )(x, sh, sc, gt, gpre, gpost, w_in, w_out, w_pool, pscale, cw)


def _even_front_sample_body(x_ref, sh_ref, sc_ref, gpre_ref, win_ref, hist_ref, cw_ref, cb_ref,
                            lng_ref, lnb_ref, a_ref, q_ref, k_ref, v_ref, u_ref):
    x = x_ref[...]
    h = _pre(x, gpre_ref[...], sc_ref[...], sh_ref[...]).astype(BF16)

    def proj(c):
        return _dot(h, win_ref[:, c * D_A:(c + 1) * D_A])

    u = proj(0) * jax.nn.sigmoid(proj(1))
    u_ref[...] = u
    q_ref[...] = proj(2) * (DH_B ** -0.5)
    k_ref[...] = proj(3)
    v_ref[...] = proj(4)

    acc = cb_ref[...] + cw_ref[CONV_A - 1:CONV_A, :] * u
    for j in range(CONV_A - 1):
        acc = acc + cw_ref[j:j + 1, :] * hist_ref[j]
    mu = jnp.mean(acc, axis=-1, keepdims=True)
    cen = acc - mu
    var = jnp.mean(cen * cen, axis=-1, keepdims=True)
    y = cen * lax.rsqrt(var + EPS) * lng_ref[...] + lnb_ref[...]
    a_ref[...] = _silu(y)


def _even_front_sample_call(x, sh, sc, gpre, w_in, hist_t, cw, cb, lng, lnb):
    n, d = x.shape
    full = lambda shape: pl.BlockSpec(shape, lambda i: (0,) * len(shape))
    out = jax.ShapeDtypeStruct((n, D_A), F32)
    return pl.pallas_call(
        _even_front_sample_body,
        out_shape=(out, out, out, out, out),
        grid=(1,),
        in_specs=[full((n, d)), full((n, d)), full((n, d)), full((1, d)), full(w_in.shape),
                  full(hist_t.shape), full(cw.shape), full((1, D_A)), full((1, D_A)), full((1, D_A))],
        out_specs=tuple(full((n, D_A)) for _ in range(5)),
        compiler_params=_params(("arbitrary",)),
        name="even_front_sample",
    )(x, sh, sc, gpre, w_in, hist_t, cw, cb, lng, lnb)


def _decode_attn_body(pt_ref, q_ref, kn_ref, vn_ref, lamqk_ref, g_ref, ck_hbm, cv_hbm, o_ref,
                      kbuf, vbuf, sem, *, lam_init, n_pages, pc, page_base):
    b = pl.program_id(0)
    nb = pl.num_programs(0)
    nc = n_pages // pc
    ahead = DECODE_SLOTS - 1
    page_rows = PAGE_SIZE * H_B
    rows = pc * page_rows
    n_comp = 2 * H_B

    def chunk_copies(seq, c, slot):
        copies = []
        for i in range(pc):
            page = page_base + pt_ref[seq * n_pages + c * pc + i]
            dst = pl.ds(i * page_rows, page_rows)
            copies.append(pltpu.make_async_copy(ck_hbm.at[page], kbuf.at[slot, dst], sem.at[0, slot]))
            copies.append(pltpu.make_async_copy(cv_hbm.at[page], vbuf.at[slot, dst], sem.at[1, slot]))
        return copies

    def start_chunk(seq, c, slot):
        for cp in chunk_copies(seq, c, slot):
            cp.start()

    def wait_chunk(seq, c, slot):
        for cp in chunk_copies(seq, c, slot):
            cp.wait()

    @pl.when(b == 0)
    def _():
        for c in range(ahead):
            start_chunk(0, c, c)

    lane = lax.broadcasted_iota(jnp.int32, (1, DV_B), 1)
    q_row, k_row, v_row = q_ref[pl.ds(b, 1), :], kn_ref[pl.ds(b, 1), :], vn_ref[pl.ds(b, 1), :]
    q_rows, k_rows, v_rows = [], [], []
    for h in range(H_B):
        cols = slice(h * DV_B, (h + 1) * DV_B)
        q_rows += [jnp.where(lane < DH_B, q_row[:, cols], 0.0), jnp.where(lane >= DH_B, q_row[:, cols], 0.0)]
        k_rows += [k_row[:, cols]] * 2
        v_rows += [v_row[:, cols]] * 2
    q8 = jnp.concatenate(q_rows, axis=0)
    k8 = jnp.concatenate(k_rows, axis=0)
    v8 = jnp.concatenate(v_rows, axis=0)
    q8b = q8.astype(BF16)

    def body(c, carry):
        m, l, acc = carry
        g = b * nc + c
        slot = lax.rem(g, DECODE_SLOTS)
        wait_chunk(b, c, slot)

        c_next = c + ahead
        wrap = c_next >= nc
        seq_next = jnp.where(wrap, b + 1, b)
        c_next = jnp.where(wrap, c_next - nc, c_next)

        @pl.when(seq_next < nb)
        def _():
            start_chunk(seq_next, c_next, lax.rem(g + ahead, DECODE_SLOTS))

        kc = kbuf[slot].astype(BF16)
        s = lax.dot_general(q8b, kc, (((1,), (1,)), ((), ())), preferred_element_type=F32)
        row_head = lax.rem(lax.broadcasted_iota(jnp.int32, s.shape, 1), H_B)
        own_head = lax.broadcasted_iota(jnp.int32, s.shape, 0) // 2
        s = jnp.where(row_head == own_head, s, -jnp.inf)
        m_new = jnp.maximum(m, jnp.max(s, axis=-1, keepdims=True))
        alpha = jnp.exp(m - m_new)
        p = jnp.exp(s - m_new)
        l = alpha * l + jnp.sum(p, axis=-1, keepdims=True)
        acc = alpha * acc + _dot(p.astype(BF16), vbuf[slot].astype(BF16))
        return m_new, l, acc

    init = (jnp.full((n_comp, 1), -jnp.inf, F32), jnp.zeros((n_comp, 1), F32),
            jnp.zeros((n_comp, DV_B), F32))
    m, l, acc = lax.fori_loop(0, nc, body, init)

    s_new = jnp.sum(q8 * k8, axis=-1, keepdims=True)
    m_fin = jnp.maximum(m, s_new)
    alpha = jnp.exp(m - m_fin)
    p_new = jnp.exp(s_new - m_fin)
    l = alpha * l + p_new
    n = (alpha * acc + p_new * v8) / l

    lam = _diff_lambda(lamqk_ref, lam_init)
    o = jnp.concatenate([n[2 * h:2 * h + 1] - lam * n[2 * h + 1:2 * h + 2] for h in range(H_B)], axis=0)
    o_ref[b] = _rms(o, g_ref[...]) * (1.0 - lam_init)


def _decode_attn_call(page_table, q, k_new, v_new, lam_qk, g, cache_k, cache_v, *, lam_init, page_base):
    nb, n_pages = page_table.shape
    pc = PAGES_PER_CHUNK
    assert n_pages % pc == 0 and DECODE_SLOTS - 1 <= n_pages // pc
    rows = pc * PAGE_SIZE * H_B
    full = lambda shape: pl.BlockSpec(shape, lambda i, pt: (0,) * len(shape))
    grid_spec = pltpu.PrefetchScalarGridSpec(
        num_scalar_prefetch=1,
        grid=(nb,),
        in_specs=[full(q.shape), full(k_new.shape), full(v_new.shape), full(lam_qk.shape), full(g.shape),
                  pl.BlockSpec(memory_space=pl.ANY), pl.BlockSpec(memory_space=pl.ANY)],
        out_specs=full((nb, H_B, DV_B)),
        scratch_shapes=[pltpu.VMEM((DECODE_SLOTS, rows, DV_B), F32),
                        pltpu.VMEM((DECODE_SLOTS, rows, DV_B), F32),
                        pltpu.SemaphoreType.DMA((2, DECODE_SLOTS))])
    return pl.pallas_call(
        functools.partial(_decode_attn_body, lam_init=lam_init, n_pages=n_pages, pc=pc, page_base=page_base),
        out_shape=jax.ShapeDtypeStruct((nb, H_B, DV_B), F32),
        grid_spec=grid_spec,
        compiler_params=_params(("arbitrary",)),
        name="diff_attn_decode",
    )(page_table.reshape(-1), q, k_new, v_new, lam_qk, g, cache_k, cache_v)


def _odd_sample_body(x_ref, sh_ref, sc_ref, gt_ref, gpre_ref, gpost_ref, win_ref, wout_ref, wpool_ref,
                     pscale_ref, cw_ref, phist_ref, dhist_ref, o_ref, u_ref, z_ref, mix_ref, *, start_pos):
    x = x_ref[...]
    h = _pre(x, gpre_ref[...], sc_ref[...], sh_ref[...]).astype(BF16)

    def proj(c):
        return _dot(h, win_ref[:, c * D_C:(c + 1) * D_C])

    u = proj(0)
    u_ref[...] = u
    bg = proj(1)
    z = proj(2) * proj(3)
    z_ref[...] = z

    for g, w in enumerate(POOL_WINDOWS):
        cols = slice(g * GC, (g + 1) * GC)
        win = u[:, cols]
        for dlt in range(1, w):
            win = win + phist_ref[POOL_HIST - dlt][:, cols]
        cnt = float(min(start_pos + 1, w))
        mdiff = (win / cnt - u[:, cols]).astype(BF16)
        mix_ref[:, cols] = (_dot(mdiff, wpool_ref[g]) * pscale_ref[:, cols]).astype(BF16)

    conv = cw_ref[CONV_D - 1:CONV_D, :] * z
    for j in range(CONV_D - 1):
        conv = conv + cw_ref[j:j + 1, :] * dhist_ref[j]
    mix_ref[:, D_C:D_C + D_D] = (bg * conv).astype(BF16)

    r = _dot(mix_ref[...], wout_ref[...])
    o_ref[...] = x + (1.0 + gt_ref[...]) * _rms(r, gpost_ref[...])


def _odd_sample_call(x, sh, sc, gt, gpre, gpost, w_in, w_out, w_pool, pscale, cw, phist_t, dhist_t,
                     *, start_pos):
    n, d = x.shape
    full = lambda shape: pl.BlockSpec(shape, lambda i: (0,) * len(shape))
    return pl.pallas_call(
        functools.partial(_odd_sample_body, start_pos=start_pos),
        out_shape=(jax.ShapeDtypeStruct((n, d), F32),
                   jax.ShapeDtypeStruct((n, D_C), F32),
                   jax.ShapeDtypeStruct((n, D_D), F32)),
        grid=(1,),
        in_specs=[full((n, d)), full((n, d)), full((n, d)), full((n, d)), full((1, d)), full((1, d)),
                  full(w_in.shape), full(w_out.shape), full(w_pool.shape), full((1, D_C)),
                  full(cw.shape), full(phist_t.shape), full(dhist_t.shape)],
        out_specs=(full((n, d)), full((n, D_C)), full((n, D_D))),
        scratch_shapes=[pltpu.VMEM((n, D_C + D_D), BF16)],
        compiler_params=_params(("arbitrary",)),
        name="odd_mixer_sample",
    )(x, sh, sc, gt, gpre, gpost, w_in, w_out, w_pool, pscale, cw, phist_t, dhist_t)


def kernel(x_prompt, x_sample, cache_k, cache_v, state_conv_a, state_pool, state_conv_d, page_table,
           c_prompt, c_sample, w_ada, b_ada, norm_g, w_ffn_in, w_ffn_out, w_in_even, w_out_even,
           conv_a_w, conv_a_b, ln_a_g, ln_a_b, lam_qk, subln_g, w_in_odd, w_out_odd, w_pool,
           pool_scale, conv_d_w):
    bp, seq, d = x_prompt.shape
    bs = x_sample.shape[0]
    assert x_sample.shape[1] == 1 and d == D_MODEL and seq % TOKEN_TILE == 0
    past_len = page_table.shape[1] * PAGE_SIZE
    nt = seq // TOKEN_TILE

    n_c = bp + bs
    pad = (-n_c) % SUBLANES
    c_all = jnp.concatenate([c_prompt, c_sample, jnp.zeros((pad, d), F32)], axis=0)
    mod = _ada_call(c_all, w_ada, b_ada)

    def mods(layer, sub):
        out_p, out_s = [], []
        for j in range(3):
            col = (sub * 3 + j) * d
            out_p.append(mod[layer, :bp, col:col + d].reshape(bp, 1, d))
            out_s.append(mod[layer, bp:n_c, col:col + d])
        return out_p, out_s

    w_ffn_in_b = w_ffn_in.astype(BF16)
    w_ffn_out_b = w_ffn_out.astype(BF16)
    w_in_even_b = w_in_even.astype(BF16)
    w_out_even_b = w_out_even.astype(BF16)
    w_in_odd_b = w_in_odd.astype(BF16)
    w_out_odd_b = w_out_odd.astype(BF16)
    w_pool_b = w_pool.astype(BF16)

    xp = x_prompt.reshape(bp * seq, d)
    xs = x_sample.reshape(bs, d)
    geo_p = dict(nb=bp, nt=nt, tm=TOKEN_TILE)
    geo_s = dict(nb=1, nt=1, tm=bs)

    def ffn(x, m, layer, sub, which, geo, name):
        sh, sc, gt = m
        return _ffn_call(x, sh, sc, gt, norm_g[layer, sub, 0][None], norm_g[layer, sub, 1][None],
                         w_ffn_in_b[layer, which], w_ffn_out_b[layer, which], resw=0.5, name=name, **geo)

    outs = {}
    for layer in range(DEPTH):
        (mp0, ms0), (mp1, ms1), (mp2, ms2) = mods(layer, 0), mods(layer, 1), mods(layer, 2)
        xp = ffn(xp, mp0, layer, 0, 0, geo_p, "ffn_prompt")
        xs = ffn(xs, ms0, layer, 0, 0, geo_s, "ffn_sample")
        gpre, gpost = norm_g[layer, 1, 0][None], norm_g[layer, 1, 1][None]
        if layer % 2 == 0:
            e = layer // 2
            lam_init = _lambda_init(layer)
            cw, cb = conv_a_w[e], conv_a_b[e][None]
            lng, lnb = ln_a_g[e][None], ln_a_b[e][None]
            a, q, kb, vt, k, v, st = _even_front_call(xp, mp1[0], mp1[1], gpre, w_in_even_b[e], cw, cb,
                                                      lng, lnb, **geo_p)
            o = _attn_call(q, kb, vt, lam_qk[e], subln_g[e][None], nb=bp, seq=seq, lam_init=lam_init)
            xp = _mixer_back_call(a, o, xp, mp1[2], gpost, w_out_even_b[e], name="even_back_prompt", **geo_p)
            outs["k_prompt"] = k.reshape(bp, seq, H_B, 2 * DH_B)
            outs["v_prompt"] = v.reshape(bp, seq, H_B, DV_B)
            outs["conv_a_prompt"] = st[:, HALO_A - (CONV_A - 1):, :]
            hist = state_conv_a[e]
            a_s, q_s, k_s, v_s, u_s = _even_front_sample_call(
                xs, ms1[0], ms1[1], gpre, w_in_even_b[e], jnp.swapaxes(hist, 0, 1), cw, cb, lng, lnb)
            n_pool = cache_k.shape[1]
            o_s = _decode_attn_call(page_table, q_s, k_s, v_s, lam_qk[e], subln_g[e][None],
                                    cache_k.reshape(-1, PAGE_SIZE * H_B, DV_B),
                                    cache_v.reshape(-1, PAGE_SIZE * H_B, DV_B),
                                    lam_init=lam_init, page_base=e * n_pool)
            xs = _mixer_back_call(a_s, o_s.reshape(bs, D_BO), xs, ms1[2], gpost, w_out_even_b[e],
                                  name="even_back_sample", **geo_s)
            outs["k_sample"] = k_s.reshape(bs, 1, H_B, 2 * DH_B)
            outs["v_sample"] = v_s.reshape(bs, 1, H_B, DV_B)
            outs["conv_a_sample"] = jnp.concatenate([hist[:, 1:], u_s[:, None, :]], axis=1)
        else:
            od = layer // 2
            xp, pst, dst = _odd_call(xp, mp1[0], mp1[1], mp1[2], gpre, gpost, w_in_odd_b[od], w_out_odd_b[od],
                                     w_pool_b[od], pool_scale[od][None], conv_d_w[od], start_pos=0, **geo_p)
            outs["pool_prompt"] = pst[:, HALO_P - POOL_HIST:, :]
            outs["conv_d_prompt"] = dst[:, HALO_D - (CONV_D - 1):, :]
            phist, dhist = state_pool[od], state_conv_d[od]
            xs, u_s, z_s = _odd_sample_call(xs, ms1[0], ms1[1], ms1[2], gpre, gpost, w_in_odd_b[od],
                                            w_out_odd_b[od], w_pool_b[od], pool_scale[od][None], conv_d_w[od],
                                            jnp.swapaxes(phist, 0, 1), jnp.swapaxes(dhist, 0, 1),
                                            start_pos=past_len)
            outs["pool_sample"] = jnp.concatenate([phist[:, 1:], u_s[:, None, :]], axis=1)
            outs["conv_d_sample"] = jnp.concatenate([dhist[:, 1:], z_s[:, None, :]], axis=1)
        xp = ffn(xp, mp2, layer, 2, 1, geo_p, "ffn_prompt")
        xs = ffn(xs, ms2, layer, 2, 1, geo_s, "ffn_sample")

    stack = lambda name: outs[name][None]
    return (xp.reshape(bp, seq, d), xs.reshape(bs, 1, d),
            stack("k_prompt"), stack("v_prompt"), stack("k_sample"), stack("v_sample"),
            stack("conv_a_prompt"), stack("conv_a_sample"),
            stack("pool_prompt"), stack("pool_sample"),
            stack("conv_d_prompt"), stack("conv_d_sample"))
```

```python
import functools
import math

import jax
import jax.numpy as jnp
from jax import lax
from jax.experimental import pallas as pl
from jax.experimental.pallas import tpu as pltpu

F32 = jnp.float32
BF16 = jnp.bfloat16

D_MODEL = 1024
DEPTH = 2
PAGE_SIZE = 128
D_A = D_MODEL // 2
CONV_A = 31
DH_B = 64
DV_B = 2 * DH_B
H_B = (D_MODEL // 2) // DV_B
D_QK = H_B * 2 * DH_B
D_BO = H_B * DV_B
D_C = D_MODEL // 2
POOL_WINDOWS = (2, 4, 8, 16)
GC = D_C // len(POOL_WINDOWS)
POOL_HIST = max(POOL_WINDOWS) - 1
D_D = D_MODEL // 2
CONV_D = 3
D_FF = ((8 * D_MODEL // 3 + 255) // 256) * 256
N_SUB = 3
EPS = 1e-6

SUBLANES = 8
LANES = 128
VMEM_LIMIT_BYTES = 56 * 1024 * 1024

TOKEN_TILE = 512
FFN_TILE = 1024
FFN_ROW_GROUP = 512
FF_CHUNK = 256
CONV_ROWS = 128
HALO_A = 32
HALO_P = 16
HALO_D = 8
QK_SCALE_LOG2 = (DH_B ** -0.5) * math.log2(math.e)
ATT_TQ = 256
ATT_TK = 512
PAGES_PER_CHUNK = 8
DECODE_SLOTS = 4


def _lambda_init(layer):
    return 0.8 - 0.6 * math.exp(-0.3 * layer)


def _params(semantics):
    return pltpu.CompilerParams(dimension_semantics=semantics, vmem_limit_bytes=VMEM_LIMIT_BYTES)


def _rms(x, g):
    return x * lax.rsqrt(jnp.mean(x * x, axis=-1, keepdims=True) + EPS) * g


def _mod(ref):
    v = ref[...]
    return v[0] if v.ndim == 3 else v


def _pre(x, g, scale, shift):
    return _rms(x, g) * (1.0 + scale) + shift


def _silu(x):
    return x * jax.nn.sigmoid(x)


def _dot(a, b):
    return jnp.dot(a, b, preferred_element_type=F32)


def _resident(shape, lead=()):
    block = (None,) * len(lead) + tuple(shape)[len(lead):]
    index = tuple(lead) + (0,) * (len(shape) - len(lead))
    return pl.BlockSpec(block, lambda *_: index, pipeline_mode=pl.Buffered(1))


def _mod_spec(arr, tm):
    if arr.ndim == 3:
        return pl.BlockSpec((1, 1, arr.shape[-1]), lambda b, t: (b, 0, 0))
    return pl.BlockSpec((tm, arr.shape[-1]), lambda b, t: (0, 0))


def _ada_body(c_ref, w_ref, b_ref, o_ref):
    c = c_ref[...]
    o_ref[0] = _dot(_silu(c).astype(BF16), w_ref[0].astype(BF16)) + b_ref[0]


def _ada_call(c_all, w_ada, b_ada):
    rows, d = c_all.shape
    depth, _, n = w_ada.shape
    tn = 1024
    return pl.pallas_call(
        _ada_body,
        out_shape=jax.ShapeDtypeStruct((depth, rows, n), F32),
        grid=(depth, n // tn),
        in_specs=[pl.BlockSpec((rows, d), lambda i, j: (0, 0)),
                  pl.BlockSpec((1, d, tn), lambda i, j: (i, 0, j)),
                  pl.BlockSpec((1, 1, tn), lambda i, j: (i, 0, j))],
        out_specs=pl.BlockSpec((1, rows, tn), lambda i, j: (i, 0, j)),
        compiler_params=_params(("parallel", "parallel")),
        name="ada_mod",
    )(c_all, w_ada, b_ada.reshape(depth, 1, n))


def _ffn_body(x_ref, sh_ref, sc_ref, gt_ref, gpre_ref, gpost_ref, win_ref, wout_ref, o_ref, act_ref,
              *, resw):
    rows = x_ref.shape[0]
    group = min(rows, FFN_ROW_GROUP)
    for r in range(rows // group):
        rs = slice(r * group, (r + 1) * group)

        def mod(ref):
            v = _mod(ref)
            return v if v.shape[0] == 1 else v[rs]

        x = x_ref[rs, :]
        h = _pre(x, gpre_ref[...], mod(sc_ref), mod(sh_ref)).astype(BF16)
        for j in range(D_FF // FF_CHUNK):
            lo = j * FF_CHUNK
            g = _dot(h, win_ref[:, lo:lo + FF_CHUNK])
            u = _dot(h, win_ref[:, D_FF + lo:D_FF + lo + FF_CHUNK])
            act_ref[rs, lo:lo + FF_CHUNK] = (_silu(g) * u).astype(BF16)
        o = _dot(act_ref[rs, :], wout_ref[...])
        o_ref[rs, :] = x + resw * (1.0 + mod(gt_ref)) * _rms(o, gpost_ref[...])


def _ffn_call(x, sh, sc, gt, gpre, gpost, w_in, w_out, *, which, resw, nb, nt, tm, name):
    n, d = x.shape
    row = pl.BlockSpec((tm, d), lambda b, t: (b * nt + t, 0))
    return pl.pallas_call(
        functools.partial(_ffn_body, resw=resw),
        out_shape=jax.ShapeDtypeStruct((n, d), F32),
        grid=(nb, nt),
        in_specs=[row, _mod_spec(sh, tm), _mod_spec(sc, tm), _mod_spec(gt, tm),
                  _resident((1, d)), _resident((1, d)),
                  _resident(w_in.shape, which), _resident(w_out.shape, which)],
        out_specs=row,
        scratch_shapes=[pltpu.VMEM((tm, D_FF), BF16)],
        compiler_params=_params(("parallel", "parallel")),
        name=name,
    )(x, sh, sc, gt, gpre, gpost, w_in, w_out)


def _even_front_body(x_ref, sh_ref, sc_ref, gpre_ref, win_ref, cw_ref, cb_ref, lng_ref, lnb_ref,
                     a_ref, qt_ref, kb_ref, vt_ref, k_ref, v_ref, st_ref, ubuf, *, tm):
    @pl.when(pl.program_id(1) == 0)
    def _():
        ubuf[0:HALO_A, :] = jnp.zeros((HALO_A, D_A), F32)

    x = x_ref[...]
    h = _pre(x, gpre_ref[...], _mod(sc_ref), _mod(sh_ref)).astype(BF16)

    def proj(c):
        return _dot(h, win_ref[:, c * D_A:(c + 1) * D_A])

    ubuf[HALO_A:HALO_A + tm, :] = proj(0) * jax.nn.sigmoid(proj(1))
    qt_ref[...] = (proj(2) * QK_SCALE_LOG2).T.astype(BF16)
    k = proj(3)
    kb_ref[...] = k.astype(BF16)
    v = proj(4)
    vt_ref[...] = v.T.astype(BF16)
    for h in range(H_B):
        k_ref[pl.ds(h, tm, stride=H_B), :] = k[:, h * DV_B:(h + 1) * DV_B]
        v_ref[pl.ds(h, tm, stride=H_B), :] = v[:, h * DV_B:(h + 1) * DV_B]

    first = HALO_A - (CONV_A - 1)
    for c in range(tm // CONV_ROWS):
        base = c * CONV_ROWS
        acc = jnp.broadcast_to(cb_ref[...], (CONV_ROWS, D_A))
        for res in range(SUBLANES):
            span = CONV_ROWS + (SUBLANES if res else 0)
            part = None
            for blk in range((first + CONV_A - 1) // SUBLANES + 1):
                j = blk * SUBLANES + res - first
                if 0 <= j < CONV_A:
                    lo = base + blk * SUBLANES
                    term = cw_ref[j:j + 1, :] * ubuf[lo:lo + span, :]
                    part = term if part is None else part + term
            acc = acc + part[res:res + CONV_ROWS, :]
        mu = jnp.mean(acc, axis=-1, keepdims=True)
        cen = acc - mu
        var = jnp.mean(cen * cen, axis=-1, keepdims=True)
        y = cen * lax.rsqrt(var + EPS) * lng_ref[...] + lnb_ref[...]
        a_ref[base:base + CONV_ROWS, :] = _silu(y).astype(BF16)

    tail = ubuf[tm:tm + HALO_A, :]
    st_ref[0] = tail
    ubuf[0:HALO_A, :] = tail


def _even_front_call(x, sh, sc, gpre, w_in, cw, cb, lng, lnb, *, nb, nt, tm):
    n, d = x.shape
    row = lambda w: pl.BlockSpec((tm, w), lambda b, t: (b * nt + t, 0))
    out_shape = (jax.ShapeDtypeStruct((n, D_A), BF16),
                 jax.ShapeDtypeStruct((D_QK, n), BF16),
                 jax.ShapeDtypeStruct((n, D_QK), BF16),
                 jax.ShapeDtypeStruct((D_BO, n), BF16),
                 jax.ShapeDtypeStruct((n * H_B, DV_B), F32),
                 jax.ShapeDtypeStruct((n * H_B, DV_B), F32),
                 jax.ShapeDtypeStruct((nb, HALO_A, D_A), F32))
    return pl.pallas_call(
        functools.partial(_even_front_body, tm=tm),
        out_shape=out_shape,
        grid=(nb, nt),
        in_specs=[row(d), _mod_spec(sh, tm), _mod_spec(sc, tm), _resident((1, d)),
                  _resident(w_in.shape), _resident(cw.shape), _resident((1, D_A)),
                  _resident((1, D_A)), _resident((1, D_A))],
        out_specs=(row(D_A),
                   pl.BlockSpec((D_QK, tm), lambda b, t: (0, b * nt + t)),
                   row(D_QK),
                   pl.BlockSpec((D_BO, tm), lambda b, t: (0, b * nt + t)),
                   pl.BlockSpec((tm * H_B, DV_B), lambda b, t: (b * nt + t, 0)),
                   pl.BlockSpec((tm * H_B, DV_B), lambda b, t: (b * nt + t, 0)),
                   pl.BlockSpec((1, HALO_A, D_A), lambda b, t: (b, 0, 0))),
        scratch_shapes=[pltpu.VMEM((HALO_A + tm, D_A), F32)],
        compiler_params=_params(("arbitrary", "arbitrary")),
        name="even_front_prompt",
    )(x, sh, sc, gpre, w_in, cw, cb, lng, lnb)


def _diff_lambda(lamqk_ref, lam_init):
    lq = lamqk_ref[...]
    s1 = jnp.sum(lq[0:1] * lq[1:2], axis=1, keepdims=True)
    s2 = jnp.sum(lq[2:3] * lq[3:4], axis=1, keepdims=True)
    return jnp.exp(s1) - jnp.exp(s2) + lam_init


def _prompt_attn_tile(qi, q_ref, k_ref, vt_ref, lam, g_ref, o_ref, qt_ref, sa_ref, sb_ref, m_ref, l_ref, acc_ref,
                      *, lam_init, tq, tk):
    q = q_ref[...]
    dim = lax.broadcasted_iota(jnp.int32, q.shape, 0)
    zero = jnp.zeros_like(q)
    qt_ref[:, 0:tq] = jnp.where(dim < DH_B, q, zero)
    qt_ref[:, tq:2 * tq] = jnp.where(dim >= DH_B, q, zero)

    m_ref[...] = jnp.full(m_ref.shape, -jnp.inf, F32)
    l_ref[...] = jnp.zeros(l_ref.shape, F32)
    acc_ref[...] = jnp.zeros(acc_ref.shape, F32)

    def key_start(kt):
        return pl.multiple_of(kt * tk, tk)

    def scores(kt):
        return _dot(k_ref[pl.ds(key_start(kt), tk), :], qt_ref[...])

    def consume(s_ref, kt, masked):
        start = key_start(kt)
        for half in range(2):
            cols = slice(half * tq, (half + 1) * tq)
            s = s_ref[:, cols]
            if masked:
                kpos = start + lax.broadcasted_iota(jnp.int32, s.shape, 0)
                qpos = qi * tq + lax.broadcasted_iota(jnp.int32, s.shape, 1)
                s = jnp.where(kpos <= qpos, s, -jnp.inf)
            m_old = m_ref[:, cols]
            m_new = jnp.maximum(m_old, jnp.max(s, axis=0, keepdims=True))
            alpha = jnp.exp2(m_old - m_new)
            p = jnp.exp2(s - m_new)
            l_ref[:, cols] = alpha * l_ref[:, cols] + jnp.sum(p, axis=0, keepdims=True)
            pv = _dot(vt_ref[:, pl.ds(start, tk)], p.astype(BF16))
            acc_ref[:, cols] = alpha * acc_ref[:, cols] + pv
            m_ref[:, cols] = m_new

    n_full = (qi * tq) // tk
    sa_ref[...] = scores(0)

    def pair(j, carry):
        sb_ref[...] = scores(2 * j + 1)
        consume(sa_ref, 2 * j, masked=False)
        sa_ref[...] = scores(2 * j + 2)
        consume(sb_ref, 2 * j + 1, masked=False)
        return carry

    lax.fori_loop(0, n_full // 2, pair, 0)

    @pl.when(n_full % 2 == 1)
    def _():
        sb_ref[...] = scores(n_full)
        consume(sa_ref, n_full - 1, masked=False)
        consume(sb_ref, n_full, masked=True)

    @pl.when(n_full % 2 == 0)
    def _():
        consume(sa_ref, n_full, masked=True)

    nt = acc_ref[...] / l_ref[...]
    ot = nt[:, 0:tq] - lam * nt[:, tq:2 * tq]
    ot = ot * lax.rsqrt(jnp.mean(ot * ot, axis=0, keepdims=True) + EPS)
    o_ref[...] = (ot.T * g_ref[...] * (1.0 - lam_init)).astype(o_ref.dtype)


def _div_rem(x, d):
    if isinstance(x, int):
        return divmod(x, d)
    return lax.div(x, d), lax.rem(x, d)


def _decode_chunk(gc, pt_ref, q_ref, kn_ref, vn_ref, lam, g_ref, ck_hbm, cv_hbm, o_ref, kbuf, vbuf, sem, st_ref,
                  *, lam_init, n_seq, n_pages, pc, page_base):
    nc = n_pages // pc
    total = n_seq * nc
    ahead = DECODE_SLOTS - 1
    page_rows = PAGE_SIZE * H_B
    n_comp = 2 * H_B

    def chunk_copies(idx):
        seq, c = _div_rem(idx, nc)
        slot = _div_rem(idx, DECODE_SLOTS)[1]
        copies = []
        for i in range(pc):
            page = page_base + pt_ref[seq * n_pages + c * pc + i]
            dst = pl.ds(i * page_rows, page_rows)
            copies.append(pltpu.make_async_copy(ck_hbm.at[page], kbuf.at[slot, dst], sem.at[0, slot]))
            copies.append(pltpu.make_async_copy(cv_hbm.at[page], vbuf.at[slot, dst], sem.at[1, slot]))
        return copies

    @pl.when(gc == 0)
    def _():
        for idx in range(ahead):
            for cp in chunk_copies(idx):
                cp.start()

    seq, c = _div_rem(gc, nc)
    slot = lax.rem(gc, DECODE_SLOTS)

    @pl.when(c == 0)
    def _():
        st_ref[0] = jnp.full((n_comp, DV_B), -jnp.inf, F32)
        st_ref[1] = jnp.zeros((n_comp, DV_B), F32)
        st_ref[2] = jnp.zeros((n_comp, DV_B), F32)

    for cp in chunk_copies(gc):
        cp.wait()

    @pl.when(gc + ahead < total)
    def _():
        for cp in chunk_copies(gc + ahead):
            cp.start()

    lane = lax.broadcasted_iota(jnp.int32, (1, DV_B), 1)
    q_row = q_ref[pl.ds(seq, 1), :]
    q_rows = []
    for h in range(H_B):
        qh = q_row[:, h * DV_B:(h + 1) * DV_B]
        q_rows += [jnp.where(lane < DH_B, qh, 0.0), jnp.where(lane >= DH_B, qh, 0.0)]
    q8 = jnp.concatenate(q_rows, axis=0)

    m, l, acc = st_ref[0][:, 0:1], st_ref[1][:, 0:1], st_ref[2]
    half = kbuf.shape[1] // 2
    zeros = jnp.zeros_like(q8)
    q16 = jnp.concatenate([jnp.concatenate([q8, zeros], axis=1),
                           jnp.concatenate([zeros, q8], axis=1)], axis=0)
    kc = jnp.concatenate([kbuf[slot, 0:half, :], kbuf[slot, half:2 * half, :]], axis=1)
    s = lax.dot_general(q16.astype(BF16), kc.astype(BF16), (((1,), (1,)), ((), ())),
                        preferred_element_type=F32)
    row_head = lax.rem(lax.broadcasted_iota(jnp.int32, s.shape, 1), H_B)
    own_head = lax.rem(lax.broadcasted_iota(jnp.int32, s.shape, 0), n_comp) // 2
    s = jnp.where(row_head == own_head, s, -jnp.inf)
    s_max = jnp.max(s, axis=-1, keepdims=True)
    m_new = jnp.maximum(m, jnp.maximum(s_max[0:n_comp], s_max[n_comp:2 * n_comp]))
    alpha = jnp.exp(m - m_new)
    p = jnp.exp(s - jnp.concatenate([m_new, m_new], axis=0))
    p_sum = jnp.sum(p, axis=-1, keepdims=True)
    l = alpha * l + p_sum[0:n_comp] + p_sum[n_comp:2 * n_comp]
    vc = jnp.concatenate([vbuf[slot, 0:half, :], vbuf[slot, half:2 * half, :]], axis=1)
    pv = _dot(p.astype(BF16), vc.astype(BF16))
    acc = alpha * acc + pv[0:n_comp, 0:DV_B] + pv[n_comp:2 * n_comp, DV_B:2 * DV_B]
    st_ref[0] = jnp.broadcast_to(m_new, (n_comp, DV_B))
    st_ref[1] = jnp.broadcast_to(l, (n_comp, DV_B))
    st_ref[2] = acc

    @pl.when(c == nc - 1)
    def _():
        k_row, v_row = kn_ref[pl.ds(seq, 1), :], vn_ref[pl.ds(seq, 1), :]
        k8 = jnp.concatenate([k_row[:, (r // 2) * DV_B:(r // 2 + 1) * DV_B] for r in range(n_comp)], axis=0)
        v8 = jnp.concatenate([v_row[:, (r // 2) * DV_B:(r // 2 + 1) * DV_B] for r in range(n_comp)], axis=0)
        s_new = jnp.sum(q8 * k8, axis=-1, keepdims=True)
        m_fin = jnp.maximum(m_new, s_new)
        a_fin = jnp.exp(m_new - m_fin)
        p_new = jnp.exp(s_new - m_fin)
        n = (a_fin * acc + p_new * v8) / (a_fin * l + p_new)
        o = jnp.concatenate([n[2 * h:2 * h + 1] - lam * n[2 * h + 1:2 * h + 2] for h in range(H_B)], axis=0)
        o_ref[seq] = _rms(o, g_ref[...]) * (1.0 - lam_init)


def _attn_body(pt_ref, q_ref, k_ref, vt_ref, lamqk_ref, g_ref, qs_ref, ks_ref, vs_ref, ck_hbm, cv_hbm,
               o_ref, os_ref, qt_ref, sa_ref, sb_ref, m_ref, l_ref, acc_ref, kbuf, vbuf, sem, st_ref,
               *, lam_init, tq, tk, chunks_per_step, decode):
    step = (pl.program_id(0) * pl.num_programs(1) + pl.program_id(1)) * pl.num_programs(2) + pl.program_id(2)
    lam = _diff_lambda(lamqk_ref, lam_init)
    for j in range(chunks_per_step):
        _decode_chunk(step * chunks_per_step + j, pt_ref, qs_ref, ks_ref, vs_ref, lam, g_ref, ck_hbm, cv_hbm,
                      os_ref, kbuf, vbuf, sem, st_ref, lam_init=lam_init, **decode)
    _prompt_attn_tile(pl.program_id(2), q_ref, k_ref, vt_ref, lam, g_ref, o_ref, qt_ref, sa_ref, sb_ref,
                      m_ref, l_ref, acc_ref, lam_init=lam_init, tq=tq, tk=tk)


def _attn_call(qt, k, vt, lam_qk, g, page_table, q_s, k_s, v_s, cache_k, cache_v, *, nb, seq, lam_init, page_base):
    n = k.shape[0]
    nq = seq // ATT_TQ
    n_seq, n_pages = page_table.shape
    pc = PAGES_PER_CHUNK
    n_steps = nb * H_B * nq
    total_chunks = n_seq * (n_pages // pc)
    assert n_pages % pc == 0 and total_chunks % n_steps == 0 and DECODE_SLOTS - 1 <= total_chunks
    rows = pc * PAGE_SIZE * H_B
    full = lambda shape: pl.BlockSpec(shape, lambda b, h, i, pt: (0,) * len(shape))
    grid_spec = pltpu.PrefetchScalarGridSpec(
        num_scalar_prefetch=1,
        grid=(nb, H_B, nq),
        in_specs=[pl.BlockSpec((DV_B, ATT_TQ), lambda b, h, i, pt: (h, b * nq + i)),
                  pl.BlockSpec((seq, DV_B), lambda b, h, i, pt: (b, h)),
                  pl.BlockSpec((DV_B, seq), lambda b, h, i, pt: (h, b)),
                  full(lam_qk.shape), full((1, DV_B)),
                  full(q_s.shape), full(k_s.shape), full(v_s.shape),
                  pl.BlockSpec(memory_space=pl.ANY), pl.BlockSpec(memory_space=pl.ANY)],
        out_specs=(pl.BlockSpec((ATT_TQ, DV_B), lambda b, h, i, pt: (b * nq + i, h)),
                   full((n_seq, H_B, DV_B))),
        scratch_shapes=[pltpu.VMEM((DV_B, 2 * ATT_TQ), BF16),
                        pltpu.VMEM((ATT_TK, 2 * ATT_TQ), F32),
                        pltpu.VMEM((ATT_TK, 2 * ATT_TQ), F32),
                        pltpu.VMEM((1, 2 * ATT_TQ), F32),
                        pltpu.VMEM((1, 2 * ATT_TQ), F32),
                        pltpu.VMEM((DV_B, 2 * ATT_TQ), F32),
                        pltpu.VMEM((DECODE_SLOTS, rows, DV_B), F32),
                        pltpu.VMEM((DECODE_SLOTS, rows, DV_B), F32),
                        pltpu.SemaphoreType.DMA((2, DECODE_SLOTS)),
                        pltpu.VMEM((3, 2 * H_B, DV_B), F32)])
    decode = dict(n_seq=n_seq, n_pages=n_pages, pc=pc, page_base=page_base)
    return pl.pallas_call(
        functools.partial(_attn_body, lam_init=lam_init, tq=ATT_TQ, tk=ATT_TK,
                          chunks_per_step=total_chunks // n_steps, decode=decode),
        out_shape=(jax.ShapeDtypeStruct((n, D_BO), BF16),
                   jax.ShapeDtypeStruct((n_seq, H_B, DV_B), F32)),
        grid_spec=grid_spec,
        compiler_params=_params(("arbitrary", "arbitrary", "arbitrary")),
        name="diff_attn",
    )(page_table.reshape(-1), qt, k, vt, lam_qk, g, q_s, k_s, v_s, cache_k, cache_v)


def _mixer_back_body(a_ref, b_ref, x_ref, gt_ref, gpost_ref, w_ref, o_ref):
    half = a_ref.shape[-1]
    r = (_dot(a_ref[...].astype(BF16), w_ref[0:half, :])
         + _dot(b_ref[...].astype(BF16), w_ref[half:2 * half, :]))
    o_ref[...] = x_ref[...] + (1.0 + _mod(gt_ref)) * _rms(r, gpost_ref[...])


def _mixer_back_call(a, b, x, gt, gpost, w_out, *, nb, nt, tm, name):
    n, d = x.shape
    row = lambda w: pl.BlockSpec((tm, w), lambda bb, t: (bb * nt + t, 0))
    return pl.pallas_call(
        _mixer_back_body,
        out_shape=jax.ShapeDtypeStruct((n, d), F32),
        grid=(nb, nt),
        in_specs=[row(a.shape[-1]), row(b.shape[-1]), row(d), _mod_spec(gt, tm),
                  _resident((1, d)), _resident(w_out.shape)],
        out_specs=row(d),
        compiler_params=_params(("parallel", "parallel")),
        name=name,
    )(a, b, x, gt, gpost, w_out)


def _odd_body(x_ref, sh_ref, sc_ref, gt_ref, gpre_ref, gpost_ref, win_ref, wout_ref, wpool_ref,
              pscale_ref, cw_ref, o_ref, pst_ref, dst_ref, pbuf, zbuf, mix_ref, *, tm, start_pos):
    t = pl.program_id(1)

    @pl.when(t == 0)
    def _():
        pbuf[0:HALO_P, :] = jnp.zeros((HALO_P, D_C), F32)
        zbuf[0:HALO_D, :] = jnp.zeros((HALO_D, D_D), F32)

    x = x_ref[...]
    h = _pre(x, gpre_ref[...], _mod(sc_ref), _mod(sh_ref)).astype(BF16)

    def proj(c):
        return _dot(h, win_ref[:, c * D_C:(c + 1) * D_C])

    u = proj(0)
    pbuf[HALO_P:HALO_P + tm, :] = u
    bg = proj(1)
    zbuf[HALO_D:HALO_D + tm, :] = proj(2) * proj(3)

    pos = start_pos + t * tm + lax.broadcasted_iota(jnp.int32, (tm, 1), 0)
    for g, w in enumerate(POOL_WINDOWS):
        cols = slice(g * GC, (g + 1) * GC)
        win = u[:, cols]
        for dlt in range(1, w):
            win = win + pbuf[HALO_P - dlt:HALO_P - dlt + tm, cols]
        cnt = jnp.minimum(pos + 1, w).astype(F32)
        mdiff = (win / cnt - u[:, cols]).astype(BF16)
        mix_ref[:, cols] = (_dot(mdiff, wpool_ref[g]) * pscale_ref[:, cols]).astype(BF16)

    first = HALO_D - (CONV_D - 1)
    conv = cw_ref[0:1, :] * zbuf[first:first + tm, :]
    for j in range(1, CONV_D):
        conv = conv + cw_ref[j:j + 1, :] * zbuf[first + j:first + j + tm, :]
    mix_ref[:, D_C:D_C + D_D] = (bg * conv).astype(BF16)

    r = _dot(mix_ref[...], wout_ref[...])
    o_ref[...] = x + (1.0 + _mod(gt_ref)) * _rms(r, gpost_ref[...])

    ptail = pbuf[tm:tm + HALO_P, :]
    pst_ref[0] = ptail
    pbuf[0:HALO_P, :] = ptail
    ztail = zbuf[tm:tm + HALO_D, :]
    dst_ref[0] = ztail
    zbuf[0:HALO_D, :] = ztail


def _odd_call(x, sh, sc, gt, gpre, gpost, w_in, w_out, w_pool, pscale, cw, *, nb, nt, tm, start_pos):
    n, d = x.shape
    row = pl.BlockSpec((tm, d), lambda b, t: (b * nt + t, 0))
    out_shape = (jax.ShapeDtypeStruct((n, d), F32),
                 jax.ShapeDtypeStruct((nb, HALO_P, D_C), F32),
                 jax.ShapeDtypeStruct((nb, HALO_D, D_D), F32))
    return pl.pallas_call(
        functools.partial(_odd_body, tm=tm, start_pos=start_pos),
        out_shape=out_shape,
        grid=(nb, nt),
        in_specs=[row, _mod_spec(sh, tm), _mod_spec(sc, tm), _mod_spec(gt, tm),
                  _resident((1, d)), _resident((1, d)), _resident(w_in.shape), _resident(w_out.shape),
                  _resident(w_pool.shape), _resident((1, D_C)), _resident(cw.shape)],
        out_specs=(row,
                   pl.BlockSpec((1, HALO_P, D_C), lambda b, t: (b, 0, 0)),
                   pl.BlockSpec((1, HALO_D, D_D), lambda b, t: (b, 0, 0))),
        scratch_shapes=[pltpu.VMEM((HALO_P + tm, D_C), F32),
                        pltpu.VMEM((HALO_D + tm, D_D), F32),
                        pltpu.VMEM((tm, D_C + D_D), BF16)],
        compiler_params=_params(("arbitrary", "arbitrary")),
        name="odd_mixer_prompt",
    )(x, sh, sc, gt, gpre, gpost, w_in, w_out, w_pool, pscale, cw)


def _even_front_sample_body(x_ref, sh_ref, sc_ref, gpre_ref, win_ref, hist_ref, cw_ref, cb_ref,
                            lng_ref, lnb_ref, a_ref, q_ref, k_ref, v_ref, u_ref):
    x = x_ref[...]
    h = _pre(x, gpre_ref[...], sc_ref[...], sh_ref[...]).astype(BF16)

    def proj(c):
        return _dot(h, win_ref[:, c * D_A:(c + 1) * D_A])

    u = proj(0) * jax.nn.sigmoid(proj(1))
    u_ref[...] = u
    q_ref[...] = proj(2) * (DH_B ** -0.5)
    k_ref[...] = proj(3)
    v_ref[...] = proj(4)

    acc = cb_ref[...] + cw_ref[CONV_A - 1:CONV_A, :] * u
    for j in range(CONV_A - 1):
        acc = acc + cw_ref[j:j + 1, :] * hist_ref[j]
    mu = jnp.mean(acc, axis=-1, keepdims=True)
    cen = acc - mu
    var = jnp.mean(cen * cen, axis=-1, keepdims=True)
    y = cen * lax.rsqrt(var + EPS) * lng_ref[...] + lnb_ref[...]
    a_ref[...] = _silu(y)


def _even_front_sample_call(x, sh, sc, gpre, w_in, hist_t, cw, cb, lng, lnb):
    n, d = x.shape
    full = lambda shape: pl.BlockSpec(shape, lambda i: (0,) * len(shape))
    out = jax.ShapeDtypeStruct((n, D_A), F32)
    return pl.pallas_call(
        _even_front_sample_body,
        out_shape=(out, out, out, out, out),
        grid=(1,),
        in_specs=[full((n, d)), full((n, d)), full((n, d)), full((1, d)), full(w_in.shape),
                  full(hist_t.shape), full(cw.shape), full((1, D_A)), full((1, D_A)), full((1, D_A))],
        out_specs=tuple(full((n, D_A)) for _ in range(5)),
        compiler_params=_params(("arbitrary",)),
        name="even_front_sample",
    )(x, sh, sc, gpre, w_in, hist_t, cw, cb, lng, lnb)


def _odd_sample_body(x_ref, sh_ref, sc_ref, gt_ref, gpre_ref, gpost_ref, win_ref, wout_ref, wpool_ref,
                     pscale_ref, cw_ref, phist_ref, dhist_ref, o_ref, u_ref, z_ref, mix_ref, *, start_pos):
    x = x_ref[...]
    h = _pre(x, gpre_ref[...], sc_ref[...], sh_ref[...]).astype(BF16)

    def proj(c):
        return _dot(h, win_ref[:, c * D_C:(c + 1) * D_C])

    u = proj(0)
    u_ref[...] = u
    bg = proj(1)
    z = proj(2) * proj(3)
    z_ref[...] = z

    for g, w in enumerate(POOL_WINDOWS):
        cols = slice(g * GC, (g + 1) * GC)
        win = u[:, cols]
        for dlt in range(1, w):
            win = win + phist_ref[POOL_HIST - dlt][:, cols]
        cnt = float(min(start_pos + 1, w))
        mdiff = (win / cnt - u[:, cols]).astype(BF16)
        mix_ref[:, cols] = (_dot(mdiff, wpool_ref[g]) * pscale_ref[:, cols]).astype(BF16)

    conv = cw_ref[CONV_D - 1:CONV_D, :] * z
    for j in range(CONV_D - 1):
        conv = conv + cw_ref[j:j + 1, :] * dhist_ref[j]
    mix_ref[:, D_C:D_C + D_D] = (bg * conv).astype(BF16)

    r = _dot(mix_ref[...], wout_ref[...])
    o_ref[...] = x + (1.0 + gt_ref[...]) * _rms(r, gpost_ref[...])


def _odd_sample_call(x, sh, sc, gt, gpre, gpost, w_in, w_out, w_pool, pscale, cw, phist_t, dhist_t,
                     *, start_pos):
    n, d = x.shape
    full = lambda shape: pl.BlockSpec(shape, lambda i: (0,) * len(shape))
    return pl.pallas_call(
        functools.partial(_odd_sample_body, start_pos=start_pos),
        out_shape=(jax.ShapeDtypeStruct((n, d), F32),
                   jax.ShapeDtypeStruct((n, D_C), F32),
                   jax.ShapeDtypeStruct((n, D_D), F32)),
        grid=(1,),
        in_specs=[full((n, d)), full((n, d)), full((n, d)), full((n, d)), full((1, d)), full((1, d)),
                  full(w_in.shape), full(w_out.shape), full(w_pool.shape), full((1, D_C)),
                  full(cw.shape), full(phist_t.shape), full(dhist_t.shape)],
        out_specs=(full((n, d)), full((n, D_C)), full((n, D_D))),
        scratch_shapes=[pltpu.VMEM((n, D_C + D_D), BF16)],
        compiler_params=_params(("arbitrary",)),
        name="odd_mixer_sample",
    )(x, sh, sc, gt, gpre, gpost, w_in, w_out, w_pool, pscale, cw, phist_t, dhist_t)


def kernel(x_prompt, x_sample, cache_k, cache_v, state_conv_a, state_pool, state_conv_d, page_table,
           c_prompt, c_sample, w_ada, b_ada, norm_g, w_ffn_in, w_ffn_out, w_in_even, w_out_even,
           conv_a_w, conv_a_b, ln_a_g, ln_a_b, lam_qk, subln_g, w_in_odd, w_out_odd, w_pool,
           pool_scale, conv_d_w):
    bp, seq, d = x_prompt.shape
    bs = x_sample.shape[0]
    assert x_sample.shape[1] == 1 and d == D_MODEL and seq % TOKEN_TILE == 0 and seq % FFN_TILE == 0
    past_len = page_table.shape[1] * PAGE_SIZE
    nt = seq // TOKEN_TILE

    n_c = bp + bs
    pad = (-n_c) % SUBLANES
    c_all = jnp.concatenate([c_prompt, c_sample, jnp.zeros((pad, d), F32)], axis=0)
    mod = _ada_call(c_all, w_ada, b_ada)

    def mods(layer, sub):
        out_p, out_s = [], []
        for j in range(3):
            col = (sub * 3 + j) * d
            out_p.append(mod[layer, :bp, col:col + d].reshape(bp, 1, d))
            out_s.append(mod[layer, bp:n_c, col:col + d])
        return out_p, out_s

    w_ffn_in_b = w_ffn_in.astype(BF16)
    w_ffn_out_b = w_ffn_out.astype(BF16)
    w_in_even_b = w_in_even.astype(BF16)
    w_out_even_b = w_out_even.astype(BF16)
    w_in_odd_b = w_in_odd.astype(BF16)
    w_out_odd_b = w_out_odd.astype(BF16)
    w_pool_b = w_pool.astype(BF16)

    xp = x_prompt.reshape(bp * seq, d)
    xs = x_sample.reshape(bs, d)
    geo_p = dict(nb=bp, nt=nt, tm=TOKEN_TILE)
    geo_s = dict(nb=1, nt=1, tm=bs)

    def ffn(x, m, layer, sub, which, geo, name):
        sh, sc, gt = m
        if geo is geo_p:
            geo = dict(nb=bp, nt=seq // FFN_TILE, tm=FFN_TILE)
        return _ffn_call(x, sh, sc, gt, norm_g[layer, sub, 0][None], norm_g[layer, sub, 1][None],
                         w_ffn_in_b, w_ffn_out_b, which=(layer, which), resw=0.5, name=name, **geo)

    outs = {}
    for layer in range(DEPTH):
        (mp0, ms0), (mp1, ms1), (mp2, ms2) = mods(layer, 0), mods(layer, 1), mods(layer, 2)
        xp = ffn(xp, mp0, layer, 0, 0, geo_p, "ffn_prompt")
        xs = ffn(xs, ms0, layer, 0, 0, geo_s, "ffn_sample")
        gpre, gpost = norm_g[layer, 1, 0][None], norm_g[layer, 1, 1][None]
        if layer % 2 == 0:
            e = layer // 2
            lam_init = _lambda_init(layer)
            cw, cb = conv_a_w[e], conv_a_b[e][None]
            lng, lnb = ln_a_g[e][None], ln_a_b[e][None]
            a, q, kb, vt, k, v, st = _even_front_call(xp, mp1[0], mp1[1], gpre, w_in_even_b[e], cw, cb,
                                                      lng, lnb, **geo_p)
            hist = state_conv_a[e]
            a_s, q_s, k_s, v_s, u_s = _even_front_sample_call(
                xs, ms1[0], ms1[1], gpre, w_in_even_b[e], jnp.swapaxes(hist, 0, 1), cw, cb, lng, lnb)
            o, o_s = _attn_call(q, kb, vt, lam_qk[e], subln_g[e][None], page_table, q_s, k_s, v_s,
                                cache_k.reshape(-1, PAGE_SIZE * H_B, DV_B),
                                cache_v.reshape(-1, PAGE_SIZE * H_B, DV_B),
                                nb=bp, seq=seq, lam_init=lam_init, page_base=e * cache_k.shape[1])
            xp = _mixer_back_call(a, o, xp, mp1[2], gpost, w_out_even_b[e], name="even_back_prompt", **geo_p)
            outs["k_prompt"] = k.reshape(bp, seq, H_B, 2 * DH_B)
            outs["v_prompt"] = v.reshape(bp, seq, H_B, DV_B)
            outs["conv_a_prompt"] = st[:, HALO_A - (CONV_A - 1):, :]
            xs = _mixer_back_call(a_s, o_s.reshape(bs, D_BO), xs, ms1[2], gpost, w_out_even_b[e],
                                  name="even_back_sample", **geo_s)
            outs["k_sample"] = k_s.reshape(bs, 1, H_B, 2 * DH_B)
            outs["v_sample"] = v_s.reshape(bs, 1, H_B, DV_B)
            outs["conv_a_sample"] = jnp.concatenate([hist[:, 1:], u_s[:, None, :]], axis=1)
        else:
            od = layer // 2
            xp, pst, dst = _odd_call(xp, mp1[0], mp1[1], mp1[2], gpre, gpost, w_in_odd_b[od], w_out_odd_b[od],
                                     w_pool_b[od], pool_scale[od][None], conv_d_w[od], start_pos=0, **geo_p)
            outs["pool_prompt"] = pst[:, HALO_P - POOL_HIST:, :]
            outs["conv_d_prompt"] = dst[:, HALO_D - (CONV_D - 1):, :]
            phist, dhist = state_pool[od], state_conv_d[od]
            xs, u_s, z_s = _odd_sample_call(xs, ms1[0], ms1[1], ms1[2], gpre, gpost, w_in_odd_b[od],
                                            w_out_odd_b[od], w_pool_b[od], pool_scale[od][None], conv_d_w[od],
                                            jnp.swapaxes(phist, 0, 1), jnp.swapaxes(dhist, 0, 1),
                                            start_pos=past_len)
            outs["pool_sample"] = jnp.concatenate([phist[:, 1:], u_s[:, None, :]], axis=1)
            outs["conv_d_sample"] = jnp.concatenate([dhist[:, 1:], z_s[:, None, :]], axis=1)
        xp = ffn(xp, mp2, layer, 2, 1, geo_p, "ffn_prompt")
        xs = ffn(xs, ms2, layer, 2, 1, geo_s, "ffn_sample")

    stack = lambda name: outs[name][None]
    return (xp.reshape(bp, seq, d), xs.reshape(bs, 1, d),
            stack("k_prompt"), stack("v_prompt"), stack("k_sample"), stack("v_sample"),
            stack("conv_a_prompt"), stack("conv_a_sample"),
            stack("pool_prompt"), stack("pool_sample"),
            stack("conv_d_prompt"), stack("conv_d_sample"))
```

```python
import functools
import math

import jax
import jax.numpy as jnp
from jax import lax
from jax.experimental import pallas as pl
from jax.experimental.pallas import tpu as pltpu

F32 = jnp.float32
BF16 = jnp.bfloat16

D_MODEL = 1024
DEPTH = 2
PAGE_SIZE = 128
D_A = D_MODEL // 2
CONV_A = 31
DH_B = 64
DV_B = 2 * DH_B
H_B = (D_MODEL // 2) // DV_B
D_QK = H_B * 2 * DH_B
D_BO = H_B * DV_B
D_C = D_MODEL // 2
POOL_WINDOWS = (2, 4, 8, 16)
GC = D_C // len(POOL_WINDOWS)
POOL_HIST = max(POOL_WINDOWS) - 1
D_D = D_MODEL // 2
CONV_D = 3
D_FF = ((8 * D_MODEL // 3 + 255) // 256) * 256
N_SUB = 3
EPS = 1e-6

SUBLANES = 8
LANES = 128
VMEM_LIMIT_BYTES = 56 * 1024 * 1024

TOKEN_TILE = 512
FFN_TILE = 1024
FFN_ROW_GROUP = 256
FFN_SIDE_PARTS = 2
FF_CHUNK = 256
CONV_ROWS = 128
HALO_A = 32
HALO_P = 16
HALO_D = 8
QK_SCALE_LOG2 = (DH_B ** -0.5) * math.log2(math.e)
ATT_TQ = 512
ATT_TK = 512
ONES_ROWS = 16
PAGES_PER_CHUNK = 8
DECODE_SLOTS = 4


def _lambda_init(layer):
    return 0.8 - 0.6 * math.exp(-0.3 * layer)


def _params(semantics):
    return pltpu.CompilerParams(dimension_semantics=semantics, vmem_limit_bytes=VMEM_LIMIT_BYTES)


def _rms(x, g):
    return x * lax.rsqrt(jnp.mean(x * x, axis=-1, keepdims=True) + EPS) * g


def _mod(ref):
    v = ref[...]
    return v[0] if v.ndim == 3 else v


def _unit_rms(x):
    return x * lax.rsqrt(jnp.mean(x * x, axis=-1, keepdims=True) + EPS)


def _pre(x, g, scale, shift):
    return _unit_rms(x) * (g * (1.0 + scale)) + shift


def _post(x, o, g, gate, resw=1.0):
    return x + _unit_rms(o) * (g * (resw * (1.0 + gate)))


def _silu(x):
    return x * jax.nn.sigmoid(x)


def _dot(a, b):
    return jnp.dot(a, b, preferred_element_type=F32)


def _resident(shape, lead=()):
    block = (None,) * len(lead) + tuple(shape)[len(lead):]
    index = tuple(lead) + (0,) * (len(shape) - len(lead))
    return pl.BlockSpec(block, lambda *_: index, pipeline_mode=pl.Buffered(1))


def _mod_spec(arr, tm):
    if arr.ndim == 3:
        return pl.BlockSpec((1, 1, arr.shape[-1]), lambda b, t: (b, 0, 0))
    return pl.BlockSpec((tm, arr.shape[-1]), lambda b, t: (0, 0))


def _ada_body(c_ref, w_ref, b_ref, o_ref):
    c = c_ref[...]
    o_ref[0] = _dot(_silu(c).astype(BF16), w_ref[0].astype(BF16)) + b_ref[0]


def _ada_call(c_all, w_ada, b_ada):
    rows, d = c_all.shape
    depth, _, n = w_ada.shape
    tn = 1024
    return pl.pallas_call(
        _ada_body,
        out_shape=jax.ShapeDtypeStruct((depth, rows, n), F32),
        grid=(depth, n // tn),
        in_specs=[pl.BlockSpec((rows, d), lambda i, j: (0, 0)),
                  pl.BlockSpec((1, d, tn), lambda i, j: (i, 0, j)),
                  pl.BlockSpec((1, 1, tn), lambda i, j: (i, 0, j))],
        out_specs=pl.BlockSpec((1, rows, tn), lambda i, j: (i, 0, j)),
        compiler_params=_params(("parallel", "parallel")),
        name="ada_mod",
    )(c_all, w_ada, b_ada.reshape(depth, 1, n))


def _ffn_body(x_ref, sh_ref, sc_ref, gt_ref, gpre_ref, gpost_ref, win_ref, wout_ref, o_ref,
              h_ref, act_ref, y_ref, *, resw):
    rows = x_ref.shape[0]
    group = min(rows, FFN_ROW_GROUP)
    n_groups = rows // group
    parts = FFN_SIDE_PARTS if group % (FFN_SIDE_PARTS * SUBLANES) == 0 else 1
    sub = group // parts

    def mod(ref, rs):
        v = _mod(ref)
        return v if v.shape[0] == 1 else v[rs]

    def pre_part(g, p):
        rs = slice(g * group + p * sub, g * group + (p + 1) * sub)
        h_ref[rs, :] = _pre(x_ref[rs, :], gpre_ref[...], mod(sc_ref, rs), mod(sh_ref, rs)).astype(BF16)

    def post_part(g, p):
        rs = slice(g * group + p * sub, g * group + (p + 1) * sub)
        o_ref[rs, :] = _post(x_ref[rs, :], y_ref[rs, :], gpost_ref[...], mod(gt_ref, rs), resw)

    def matmul_steps(g):
        rs = slice(g * group, (g + 1) * group)

        def up(j):
            lo = j * FF_CHUNK
            h = h_ref[rs, :]
            gate = _dot(h, win_ref[:, lo:lo + FF_CHUNK])
            val = _dot(h, win_ref[:, D_FF + lo:D_FF + lo + FF_CHUNK])
            act_ref[rs, lo:lo + FF_CHUNK] = (_silu(gate) * val).astype(BF16)

        def down():
            y_ref[rs, :] = _dot(act_ref[rs, :], wout_ref[...])

        return [functools.partial(up, j) for j in range(D_FF // FF_CHUNK)] + [down]

    for p in range(parts):
        pre_part(0, p)
    for g in range(n_groups):
        side = [functools.partial(pre_part, g + 1, p) for p in range(parts)] if g + 1 < n_groups else []
        side += [functools.partial(post_part, g - 1, p) for p in range(parts)] if g > 0 else []
        main = matmul_steps(g)
        gap = max(1, len(main) // (len(side) + 1))
        for i, step in enumerate(main):
            step()
            if side and (i + 1) % gap == 0:
                side.pop(0)()
        for rest in side:
            rest()
    for p in range(parts):
        post_part(n_groups - 1, p)


def _ffn_call(x, sh, sc, gt, gpre, gpost, w_in, w_out, *, which, resw, nb, nt, tm, name):
    n, d = x.shape
    row = pl.BlockSpec((tm, d), lambda b, t: (b * nt + t, 0))
    return pl.pallas_call(
        functools.partial(_ffn_body, resw=resw),
        out_shape=jax.ShapeDtypeStruct((n, d), F32),
        grid=(nb, nt),
        in_specs=[row, _mod_spec(sh, tm), _mod_spec(sc, tm), _mod_spec(gt, tm),
                  _resident((1, d)), _resident((1, d)),
                  _resident(w_in.shape, which), _resident(w_out.shape, which)],
        out_specs=row,
        scratch_shapes=[pltpu.VMEM((tm, d), BF16), pltpu.VMEM((tm, D_FF), BF16), pltpu.VMEM((tm, d), F32)],
        compiler_params=_params(("parallel", "parallel")),
        name=name,
    )(x, sh, sc, gt, gpre, gpost, w_in, w_out)


def _even_front_body(x_ref, sh_ref, sc_ref, gpre_ref, win_ref, cw_ref, cb_ref, lng_ref, lnb_ref,
                     a_ref, qt_ref, kb_ref, vt_ref, k_ref, v_ref, st_ref, ubuf, *, tm):
    @pl.when(pl.program_id(1) == 0)
    def _():
        ubuf[0:HALO_A, :] = jnp.zeros((HALO_A, D_A), F32)

    h = _pre(x_ref[...], gpre_ref[...], _mod(sc_ref), _mod(sh_ref)).astype(BF16)

    def proj(c):
        return _dot(h, win_ref[:, c * D_A:(c + 1) * D_A])

    ubuf[HALO_A:HALO_A + tm, :] = proj(0) * jax.nn.sigmoid(proj(1))
    qt_ref[...] = (proj(2) * QK_SCALE_LOG2).T.astype(BF16)
    k = proj(3)
    kb_ref[...] = k.astype(BF16)
    v = proj(4)
    vt_ref[...] = v.T.astype(BF16)
    for hd in range(H_B):
        k_ref[pl.ds(hd, tm, stride=H_B), :] = k[:, hd * DV_B:(hd + 1) * DV_B]
        v_ref[pl.ds(hd, tm, stride=H_B), :] = v[:, hd * DV_B:(hd + 1) * DV_B]

    first = HALO_A - (CONV_A - 1)
    for c in range(tm // CONV_ROWS):
        base = c * CONV_ROWS
        acc = jnp.broadcast_to(cb_ref[...], (CONV_ROWS, D_A))
        for res in range(SUBLANES):
            span = CONV_ROWS + (SUBLANES if res else 0)
            part = None
            for blk in range((first + CONV_A - 1) // SUBLANES + 1):
                j = blk * SUBLANES + res - first
                if 0 <= j < CONV_A:
                    lo = base + blk * SUBLANES
                    term = cw_ref[j:j + 1, :] * ubuf[lo:lo + span, :]
                    part = term if part is None else part + term
            acc = acc + part[res:res + CONV_ROWS, :]
        mu = jnp.mean(acc, axis=-1, keepdims=True)
        cen = acc - mu
        var = jnp.mean(cen * cen, axis=-1, keepdims=True)
        y = cen * lax.rsqrt(var + EPS) * lng_ref[...] + lnb_ref[...]
        a_ref[base:base + CONV_ROWS, :] = _silu(y).astype(BF16)

    tail = ubuf[tm:tm + HALO_A, :]
    st_ref[0] = tail
    ubuf[0:HALO_A, :] = tail


def _even_front_call(x, sh, sc, gpre, w_in, cw, cb, lng, lnb, *, nb, nt, tm):
    n, d = x.shape
    row = lambda w: pl.BlockSpec((tm, w), lambda b, t: (b * nt + t, 0))
    out_shape = (jax.ShapeDtypeStruct((n, D_A), BF16),
                 jax.ShapeDtypeStruct((D_QK, n), BF16),
                 jax.ShapeDtypeStruct((n, D_QK), BF16),
                 jax.ShapeDtypeStruct((D_BO, n), BF16),
                 jax.ShapeDtypeStruct((n * H_B, DV_B), F32),
                 jax.ShapeDtypeStruct((n * H_B, DV_B), F32),
                 jax.ShapeDtypeStruct((nb, HALO_A, D_A), F32))
    return pl.pallas_call(
        functools.partial(_even_front_body, tm=tm),
        out_shape=out_shape,
        grid=(nb, nt),
        in_specs=[row(d), _mod_spec(sh, tm), _mod_spec(sc, tm), _resident((1, d)),
                  _resident(w_in.shape), _resident(cw.shape), _resident((1, D_A)),
                  _resident((1, D_A)), _resident((1, D_A))],
        out_specs=(row(D_A),
                   pl.BlockSpec((D_QK, tm), lambda b, t: (0, b * nt + t)),
                   row(D_QK),
                   pl.BlockSpec((D_BO, tm), lambda b, t: (0, b * nt + t)),
                   pl.BlockSpec((tm * H_B, DV_B), lambda b, t: (b * nt + t, 0)),
                   pl.BlockSpec((tm * H_B, DV_B), lambda b, t: (b * nt + t, 0)),
                   pl.BlockSpec((1, HALO_A, D_A), lambda b, t: (b, 0, 0))),
        scratch_shapes=[pltpu.VMEM((HALO_A + tm, D_A), F32)],
        compiler_params=_params(("arbitrary", "arbitrary")),
        name="even_front_prompt",
    )(x, sh, sc, gpre, w_in, cw, cb, lng, lnb)


def _diff_lambda(lamqk_ref, lam_init):
    lq = lamqk_ref[...]
    s1 = jnp.sum(lq[0:1] * lq[1:2], axis=1, keepdims=True)
    s2 = jnp.sum(lq[2:3] * lq[3:4], axis=1, keepdims=True)
    return jnp.exp(s1) - jnp.exp(s2) + lam_init


def _prompt_attn_tile(qi, q_ref, k_ref, vx_ref, lam, g_ref, o_ref, qt_ref, sa_ref, sb_ref, m_ref, acc_ref,
                      *, lam_init, tq, tk):
    q = q_ref[...]
    dim = lax.broadcasted_iota(jnp.int32, q.shape, 0)
    zero = jnp.zeros_like(q)
    qt_ref[:, 0:tq] = jnp.where(dim < DH_B, q, zero)
    qt_ref[:, tq:2 * tq] = jnp.where(dim >= DH_B, q, zero)

    m_ref[...] = jnp.full(m_ref.shape, -jnp.inf, F32)
    acc_ref[...] = jnp.zeros(acc_ref.shape, F32)

    def key_start(kt):
        return pl.multiple_of(kt * tk, tk)

    def scores(kt):
        return _dot(k_ref[pl.ds(key_start(kt), tk), :], qt_ref[...])

    def consume(s_ref, kt, masked):
        start = key_start(kt)
        for half in range(2):
            cols = slice(half * tq, (half + 1) * tq)
            s = s_ref[:, cols]
            if masked:
                kpos = start + lax.broadcasted_iota(jnp.int32, s.shape, 0)
                qpos = qi * tq + lax.broadcasted_iota(jnp.int32, s.shape, 1)
                s = jnp.where(kpos <= qpos, s, -jnp.inf)
            m_old = m_ref[:, cols]
            m_new = jnp.maximum(m_old, jnp.max(s, axis=0, keepdims=True))
            alpha = jnp.exp2(m_old - m_new)
            p = jnp.exp2(s - m_new).astype(BF16)
            pv = _dot(vx_ref[:, pl.ds(start, tk)], p)
            acc_ref[:, cols] = alpha * acc_ref[:, cols] + pv
            m_ref[:, cols] = m_new

    n_full = (qi * tq) // tk
    sa_ref[...] = scores(0)

    def pair(j, carry):
        sb_ref[...] = scores(2 * j + 1)
        consume(sa_ref, 2 * j, masked=False)
        sa_ref[...] = scores(2 * j + 2)
        consume(sb_ref, 2 * j + 1, masked=False)
        return carry

    lax.fori_loop(0, n_full // 2, pair, 0)

    @pl.when(n_full % 2 == 1)
    def _():
        sb_ref[...] = scores(n_full)
        consume(sa_ref, n_full - 1, masked=False)
        consume(sb_ref, n_full, masked=True)

    @pl.when(n_full % 2 == 0)
    def _():
        consume(sa_ref, n_full, masked=True)

    nt = acc_ref[0:DV_B, :] / acc_ref[DV_B:DV_B + 1, :]
    ot = nt[:, 0:tq] - lam * nt[:, tq:2 * tq]
    ot = ot * lax.rsqrt(jnp.mean(ot * ot, axis=0, keepdims=True) + EPS)
    o_ref[...] = (ot.T * g_ref[...] * (1.0 - lam_init)).astype(o_ref.dtype)


def _div_rem(x, d):
    if isinstance(x, int):
        return divmod(x, d)
    return lax.div(x, d), lax.rem(x, d)


def _decode_chunk(gc, pt_ref, q_ref, kn_ref, vn_ref, lam, g_ref, ck_hbm, cv_hbm, o_ref, kbuf, vbuf, sem, st_ref,
                  *, lam_init, n_seq, n_pages, pc, page_base):
    nc = n_pages // pc
    total = n_seq * nc
    ahead = DECODE_SLOTS - 1
    page_rows = PAGE_SIZE * H_B
    n_comp = 2 * H_B

    def chunk_copies(idx):
        seq, c = _div_rem(idx, nc)
        slot = _div_rem(idx, DECODE_SLOTS)[1]
        copies = []
        for i in range(pc):
            page = page_base + pt_ref[seq * n_pages + c * pc + i]
            dst = pl.ds(i * page_rows, page_rows)
            copies.append(pltpu.make_async_copy(ck_hbm.at[page], kbuf.at[slot, dst], sem.at[0, slot]))
            copies.append(pltpu.make_async_copy(cv_hbm.at[page], vbuf.at[slot, dst], sem.at[1, slot]))
        return copies

    @pl.when(gc == 0)
    def _():
        for idx in range(ahead):
            for cp in chunk_copies(idx):
                cp.start()

    seq, c = _div_rem(gc, nc)
    slot = lax.rem(gc, DECODE_SLOTS)

    @pl.when(c == 0)
    def _():
        st_ref[0] = jnp.full((n_comp, DV_B), -jnp.inf, F32)
        st_ref[1] = jnp.zeros((n_comp, DV_B), F32)
        st_ref[2] = jnp.zeros((n_comp, DV_B), F32)

    for cp in chunk_copies(gc):
        cp.wait()

    @pl.when(gc + ahead < total)
    def _():
        for cp in chunk_copies(gc + ahead):
            cp.start()

    lane = lax.broadcasted_iota(jnp.int32, (1, DV_B), 1)
    q_row = q_ref[pl.ds(seq, 1), :]
    q_rows = []
    for h in range(H_B):
        qh = q_row[:, h * DV_B:(h + 1) * DV_B]
        q_rows += [jnp.where(lane < DH_B, qh, 0.0), jnp.where(lane >= DH_B, qh, 0.0)]
    q8 = jnp.concatenate(q_rows, axis=0)

    m, l, acc = st_ref[0][:, 0:1], st_ref[1][:, 0:1], st_ref[2]
    half = kbuf.shape[1] // 2
    zeros = jnp.zeros_like(q8)
    q16 = jnp.concatenate([jnp.concatenate([q8, zeros], axis=1),
                           jnp.concatenate([zeros, q8], axis=1)], axis=0)
    kc = jnp.concatenate([kbuf[slot, 0:half, :], kbuf[slot, half:2 * half, :]], axis=1)
    s = lax.dot_general(q16.astype(BF16), kc.astype(BF16), (((1,), (1,)), ((), ())),
                        preferred_element_type=F32)
    row_head = lax.rem(lax.broadcasted_iota(jnp.int32, s.shape, 1), H_B)
    own_head = lax.rem(lax.broadcasted_iota(jnp.int32, s.shape, 0), n_comp) // 2
    s = jnp.where(row_head == own_head, s, -jnp.inf)
    s_max = jnp.max(s, axis=-1, keepdims=True)
    m_new = jnp.maximum(m, jnp.maximum(s_max[0:n_comp], s_max[n_comp:2 * n_comp]))
    alpha = jnp.exp(m - m_new)
    p = jnp.exp(s - jnp.concatenate([m_new, m_new], axis=0))
    p_sum = jnp.sum(p, axis=-1, keepdims=True)
    l = alpha * l + p_sum[0:n_comp] + p_sum[n_comp:2 * n_comp]
    vc = jnp.concatenate([vbuf[slot, 0:half, :], vbuf[slot, half:2 * half, :]], axis=1)
    pv = _dot(p.astype(BF16), vc.astype(BF16))
    acc = alpha * acc + pv[0:n_comp, 0:DV_B] + pv[n_comp:2 * n_comp, DV_B:2 * DV_B]
    st_ref[0] = jnp.broadcast_to(m_new, (n_comp, DV_B))
    st_ref[1] = jnp.broadcast_to(l, (n_comp, DV_B))
    st_ref[2] = acc

    @pl.when(c == nc - 1)
    def _():
        k_row, v_row = kn_ref[pl.ds(seq, 1), :], vn_ref[pl.ds(seq, 1), :]
        k8 = jnp.concatenate([k_row[:, (r // 2) * DV_B:(r // 2 + 1) * DV_B] for r in range(n_comp)], axis=0)
        v8 = jnp.concatenate([v_row[:, (r // 2) * DV_B:(r // 2 + 1) * DV_B] for r in range(n_comp)], axis=0)
        s_new = jnp.sum(q8 * k8, axis=-1, keepdims=True)
        m_fin = jnp.maximum(m_new, s_new)
        a_fin = jnp.exp(m_new - m_fin)
        p_new = jnp.exp(s_new - m_fin)
        n = (a_fin * acc + p_new * v8) / (a_fin * l + p_new)
        o = jnp.concatenate([n[2 * h:2 * h + 1] - lam * n[2 * h + 1:2 * h + 2] for h in range(H_B)], axis=0)
        o_ref[seq] = _rms(o, g_ref[...]) * (1.0 - lam_init)


def _attn_body(pt_ref, q_ref, k_ref, vt_ref, lamqk_ref, g_ref, qs_ref, ks_ref, vs_ref, ck_hbm, cv_hbm,
               o_ref, os_ref, qt_ref, sa_ref, sb_ref, m_ref, acc_ref, vx_ref, kbuf, vbuf, sem, st_ref,
               *, lam_init, tq, tk, chunks_per_step, decode):
    qi = pl.program_id(2)
    step = (pl.program_id(0) * pl.num_programs(1) + pl.program_id(1)) * pl.num_programs(2) + qi
    lam = _diff_lambda(lamqk_ref, lam_init)
    for j in range(chunks_per_step):
        _decode_chunk(step * chunks_per_step + j, pt_ref, qs_ref, ks_ref, vs_ref, lam, g_ref, ck_hbm, cv_hbm,
                      os_ref, kbuf, vbuf, sem, st_ref, lam_init=lam_init, **decode)

    @pl.when(qi == 0)
    def _():
        vx_ref[0:DV_B, :] = vt_ref[...]
        vx_ref[DV_B:DV_B + ONES_ROWS, :] = jnp.ones((ONES_ROWS, vx_ref.shape[1]), BF16)

    _prompt_attn_tile(qi, q_ref, k_ref, vx_ref, lam, g_ref, o_ref, qt_ref, sa_ref, sb_ref,
                      m_ref, acc_ref, lam_init=lam_init, tq=tq, tk=tk)


def _attn_call(qt, k, vt, lam_qk, g, page_table, q_s, k_s, v_s, cache_k, cache_v, *, nb, seq, lam_init, page_base):
    n = k.shape[0]
    nq = seq // ATT_TQ
    n_seq, n_pages = page_table.shape
    pc = PAGES_PER_CHUNK
    n_steps = nb * H_B * nq
    total_chunks = n_seq * (n_pages // pc)
    assert n_pages % pc == 0 and total_chunks % n_steps == 0 and DECODE_SLOTS - 1 <= total_chunks
    rows = pc * PAGE_SIZE * H_B
    full = lambda shape: pl.BlockSpec(shape, lambda b, h, i, pt: (0,) * len(shape))
    grid_spec = pltpu.PrefetchScalarGridSpec(
        num_scalar_prefetch=1,
        grid=(nb, H_B, nq),
        in_specs=[pl.BlockSpec((DV_B, ATT_TQ), lambda b, h, i, pt: (h, b * nq + i)),
                  pl.BlockSpec((seq, DV_B), lambda b, h, i, pt: (b, h)),
                  pl.BlockSpec((DV_B, seq), lambda b, h, i, pt: (h, b)),
                  full(lam_qk.shape), full((1, DV_B)),
                  full(q_s.shape), full(k_s.shape), full(v_s.shape),
                  pl.BlockSpec(memory_space=pl.ANY), pl.BlockSpec(memory_space=pl.ANY)],
        out_specs=(pl.BlockSpec((ATT_TQ, DV_B), lambda b, h, i, pt: (b * nq + i, h)),
                   full((n_seq, H_B, DV_B))),
        scratch_shapes=[pltpu.VMEM((DV_B, 2 * ATT_TQ), BF16),
                        pltpu.VMEM((ATT_TK, 2 * ATT_TQ), F32),
                        pltpu.VMEM((ATT_TK, 2 * ATT_TQ), F32),
                        pltpu.VMEM((1, 2 * ATT_TQ), F32),
                        pltpu.VMEM((DV_B + ONES_ROWS, 2 * ATT_TQ), F32),
                        pltpu.VMEM((DV_B + ONES_ROWS, seq), BF16),
                        pltpu.VMEM((DECODE_SLOTS, rows, DV_B), F32),
                        pltpu.VMEM((DECODE_SLOTS, rows, DV_B), F32),
                        pltpu.SemaphoreType.DMA((2, DECODE_SLOTS)),
                        pltpu.VMEM((3, 2 * H_B, DV_B), F32)])
    decode = dict(n_seq=n_seq, n_pages=n_pages, pc=pc, page_base=page_base)
    return pl.pallas_call(
        functools.partial(_attn_body, lam_init=lam_init, tq=ATT_TQ, tk=ATT_TK,
                          chunks_per_step=total_chunks // n_steps, decode=decode),
        out_shape=(jax.ShapeDtypeStruct((n, D_BO), BF16),
                   jax.ShapeDtypeStruct((n_seq, H_B, DV_B), F32)),
        grid_spec=grid_spec,
        compiler_params=_params(("arbitrary", "arbitrary", "arbitrary")),
        name="diff_attn",
    )(page_table.reshape(-1), qt, k, vt, lam_qk, g, q_s, k_s, v_s, cache_k, cache_v)


def _mixer_back_body(a_ref, b_ref, x_ref, gt_ref, gpost_ref, w_ref, o_ref):
    half = a_ref.shape[-1]
    r = (_dot(a_ref[...].astype(BF16), w_ref[0:half, :])
         + _dot(b_ref[...].astype(BF16), w_ref[half:2 * half, :]))
    o_ref[...] = _post(x_ref[...], r, gpost_ref[...], _mod(gt_ref))


def _mixer_back_call(a, b, x, gt, gpost, w_out, *, nb, nt, tm, name):
    n, d = x.shape
    row = lambda w: pl.BlockSpec((tm, w), lambda bb, t: (bb * nt + t, 0))
    return pl.pallas_call(
        _mixer_back_body,
        out_shape=jax.ShapeDtypeStruct((n, d), F32),
        grid=(nb, nt),
        in_specs=[row(a.shape[-1]), row(b.shape[-1]), row(d), _mod_spec(gt, tm),
                  _resident((1, d)), _resident(w_out.shape)],
        out_specs=row(d),
        compiler_params=_params(("parallel", "parallel")),
        name=name,
    )(a, b, x, gt, gpost, w_out)


def _odd_body(x_ref, sh_ref, sc_ref, gt_ref, gpre_ref, gpost_ref, win_ref, wout_ref, wpool_ref,
              pscale_ref, cw_ref, o_ref, pst_ref, dst_ref, pbuf, zbuf, mix_ref, *, tm, start_pos):
    t = pl.program_id(1)

    @pl.when(t == 0)
    def _():
        pbuf[0:HALO_P, :] = jnp.zeros((HALO_P, D_C), F32)
        zbuf[0:HALO_D, :] = jnp.zeros((HALO_D, D_D), F32)

    x = x_ref[...]
    h = _pre(x, gpre_ref[...], _mod(sc_ref), _mod(sh_ref)).astype(BF16)

    def proj(c):
        return _dot(h, win_ref[:, c * D_C:(c + 1) * D_C])

    u = proj(0)
    pbuf[HALO_P:HALO_P + tm, :] = u
    bg = proj(1)
    zbuf[HALO_D:HALO_D + tm, :] = proj(2) * proj(3)

    pos = start_pos + t * tm + lax.broadcasted_iota(jnp.int32, (tm, 1), 0)
    for g, w in enumerate(POOL_WINDOWS):
        cols = slice(g * GC, (g + 1) * GC)
        win = u[:, cols]
        for dlt in range(1, w):
            win = win + pbuf[HALO_P - dlt:HALO_P - dlt + tm, cols]
        cnt = jnp.minimum(pos + 1, w).astype(F32)
        mdiff = (win / cnt - u[:, cols]).astype(BF16)
        mix_ref[:, cols] = (_dot(mdiff, wpool_ref[g]) * pscale_ref[:, cols]).astype(BF16)

    first = HALO_D - (CONV_D - 1)
    conv = cw_ref[0:1, :] * zbuf[first:first + tm, :]
    for j in range(1, CONV_D):
        conv = conv + cw_ref[j:j + 1, :] * zbuf[first + j:first + j + tm, :]
    mix_ref[:, D_C:D_C + D_D] = (bg * conv).astype(BF16)

    r = _dot(mix_ref[...], wout_ref[...])
    o_ref[...] = _post(x, r, gpost_ref[...], _mod(gt_ref))

    ptail = pbuf[tm:tm + HALO_P, :]
    pst_ref[0] = ptail
    pbuf[0:HALO_P, :] = ptail
    ztail = zbuf[tm:tm + HALO_D, :]
    dst_ref[0] = ztail
    zbuf[0:HALO_D, :] = ztail


def _odd_call(x, sh, sc, gt, gpre, gpost, w_in, w_out, w_pool, pscale, cw, *, nb, nt, tm, start_pos):
    n, d = x.shape
    row = pl.BlockSpec((tm, d), lambda b, t: (b * nt + t, 0))
    out_shape = (jax.ShapeDtypeStruct((n, d), F32),
                 jax.ShapeDtypeStruct((nb, HALO_P, D_C), F32),
                 jax.ShapeDtypeStruct((nb, HALO_D, D_D), F32))
    return pl.pallas_call(
        functools.partial(_odd_body, tm=tm, start_pos=start_pos),
        out_shape=out_shape,
        grid=(nb, nt),
        in_specs=[row, _mod_spec(sh, tm), _mod_spec(sc, tm), _mod_spec(gt, tm),
                  _resident((1, d)), _resident((1, d)), _resident(w_in.shape), _resident(w_out.shape),
                  _resident(w_pool.shape), _resident((1, D_C)), _resident(cw.shape)],
        out_specs=(row,
                   pl.BlockSpec((1, HALO_P, D_C), lambda b, t: (b, 0, 0)),
                   pl.BlockSpec((1, HALO_D, D_D), lambda b, t: (b, 0, 0))),
        scratch_shapes=[pltpu.VMEM((HALO_P + tm, D_C), F32),
                        pltpu.VMEM((HALO_D + tm, D_D), F32),
                        pltpu.VMEM((tm, D_C + D_D), BF16)],
        compiler_params=_params(("arbitrary", "arbitrary")),
        name="odd_mixer_prompt",
    )(x, sh, sc, gt, gpre, gpost, w_in, w_out, w_pool, pscale, cw)


def _even_front_sample_body(x_ref, sh_ref, sc_ref, gpre_ref, win_ref, hist_ref, cw_ref, cb_ref,
                            lng_ref, lnb_ref, a_ref, q_ref, k_ref, v_ref, u_ref):
    x = x_ref[...]
    h = _pre(x, gpre_ref[...], sc_ref[...], sh_ref[...]).astype(BF16)

    def proj(c):
        return _dot(h, win_ref[:, c * D_A:(c + 1) * D_A])

    u = proj(0) * jax.nn.sigmoid(proj(1))
    u_ref[...] = u
    q_ref[...] = proj(2) * (DH_B ** -0.5)
    k_ref[...] = proj(3)
    v_ref[...] = proj(4)

    acc = cb_ref[...] + cw_ref[CONV_A - 1:CONV_A, :] * u
    for j in range(CONV_A - 1):
        acc = acc + cw_ref[j:j + 1, :] * hist_ref[j]
    mu = jnp.mean(acc, axis=-1, keepdims=True)
    cen = acc - mu
    var = jnp.mean(cen * cen, axis=-1, keepdims=True)
    y = cen * lax.rsqrt(var + EPS) * lng_ref[...] + lnb_ref[...]
    a_ref[...] = _silu(y)


def _even_front_sample_call(x, sh, sc, gpre, w_in, hist_t, cw, cb, lng, lnb):
    n, d = x.shape
    full = lambda shape: pl.BlockSpec(shape, lambda i: (0,) * len(shape))
    out = jax.ShapeDtypeStruct((n, D_A), F32)
    return pl.pallas_call(
        _even_front_sample_body,
        out_shape=(out, out, out, out, out),
        grid=(1,),
        in_specs=[full((n, d)), full((n, d)), full((n, d)), full((1, d)), full(w_in.shape),
                  full(hist_t.shape), full(cw.shape), full((1, D_A)), full((1, D_A)), full((1, D_A))],
        out_specs=tuple(full((n, D_A)) for _ in range(5)),
        compiler_params=_params(("arbitrary",)),
        name="even_front_sample",
    )(x, sh, sc, gpre, w_in, hist_t, cw, cb, lng, lnb)


def _odd_sample_body(x_ref, sh_ref, sc_ref, gt_ref, gpre_ref, gpost_ref, win_ref, wout_ref, wpool_ref,
                     pscale_ref, cw_ref, phist_ref, dhist_ref, o_ref, u_ref, z_ref, mix_ref, *, start_pos):
    x = x_ref[...]
    h = _pre(x, gpre_ref[...], sc_ref[...], sh_ref[...]).astype(BF16)

    def proj(c):
        return _dot(h, win_ref[:, c * D_C:(c + 1) * D_C])

    u = proj(0)
    u_ref[...] = u
    bg = proj(1)
    z = proj(2) * proj(3)
    z_ref[...] = z

    for g, w in enumerate(POOL_WINDOWS):
        cols = slice(g * GC, (g + 1) * GC)
        win = u[:, cols]
        for dlt in range(1, w):
            win = win + phist_ref[POOL_HIST - dlt][:, cols]
        cnt = float(min(start_pos + 1, w))
        mdiff = (win / cnt - u[:, cols]).astype(BF16)
        mix_ref[:, cols] = (_dot(mdiff, wpool_ref[g]) * pscale_ref[:, cols]).astype(BF16)

    conv = cw_ref[CONV_D - 1:CONV_D, :] * z
    for j in range(CONV_D - 1):
        conv = conv + cw_ref[j:j + 1, :] * dhist_ref[j]
    mix_ref[:, D_C:D_C + D_D] = (bg * conv).astype(BF16)

    r = _dot(mix_ref[...], wout_ref[...])
    o_ref[...] = _post(x, r, gpost_ref[...], gt_ref[...])


def _odd_sample_call(x, sh, sc, gt, gpre, gpost, w_in, w_out, w_pool, pscale, cw, phist_t, dhist_t,
                     *, start_pos):
    n, d = x.shape
    full = lambda shape: pl.BlockSpec(shape, lambda i: (0,) * len(shape))
    return pl.pallas_call(
        functools.partial(_odd_sample_body, start_pos=start_pos),
        out_shape=(jax.ShapeDtypeStruct((n, d), F32),
                   jax.ShapeDtypeStruct((n, D_C), F32),
                   jax.ShapeDtypeStruct((n, D_D), F32)),
        grid=(1,),
        in_specs=[full((n, d)), full((n, d)), full((n, d)), full((n, d)), full((1, d)), full((1, d)),
                  full(w_in.shape), full(w_out.shape), full(w_pool.shape), full((1, D_C)),
                  full(cw.shape), full(phist_t.shape), full(dhist_t.shape)],
        out_specs=(full((n, d)), full((n, D_C)), full((n, D_D))),
        scratch_shapes=[pltpu.VMEM((n, D_C + D_D), BF16)],
        compiler_params=_params(("arbitrary",)),
        name="odd_mixer_sample",
    )(x, sh, sc, gt, gpre, gpost, w_in, w_out, w_pool, pscale, cw, phist_t, dhist_t)


def kernel(x_prompt, x_sample, cache_k, cache_v, state_conv_a, state_pool, state_conv_d, page_table,
           c_prompt, c_sample, w_ada, b_ada, norm_g, w_ffn_in, w_ffn_out, w_in_even, w_out_even,
           conv_a_w, conv_a_b, ln_a_g, ln_a_b, lam_qk, subln_g, w_in_odd, w_out_odd, w_pool,
           pool_scale, conv_d_w):
    bp, seq, d = x_prompt.shape
    bs = x_sample.shape[0]
    assert x_sample.shape[1] == 1 and d == D_MODEL and seq % TOKEN_TILE == 0 and seq % FFN_TILE == 0
    past_len = page_table.shape[1] * PAGE_SIZE
    nt = seq // TOKEN_TILE

    n_c = bp + bs
    pad = (-n_c) % SUBLANES
    c_all = jnp.concatenate([c_prompt, c_sample, jnp.zeros((pad, d), F32)], axis=0)
    mod = _ada_call(c_all, w_ada, b_ada)

    def mods(layer, sub):
        out_p, out_s = [], []
        for j in range(3):
            col = (sub * 3 + j) * d
            out_p.append(mod[layer, :bp, col:col + d].reshape(bp, 1, d))
            out_s.append(mod[layer, bp:n_c, col:col + d])
        return out_p, out_s

    w_ffn_in_b = w_ffn_in.astype(BF16)
    w_ffn_out_b = w_ffn_out.astype(BF16)
    w_in_even_b = w_in_even.astype(BF16)
    w_out_even_b = w_out_even.astype(BF16)
    w_in_odd_b = w_in_odd.astype(BF16)
    w_out_odd_b = w_out_odd.astype(BF16)
    w_pool_b = w_pool.astype(BF16)

    xp = x_prompt.reshape(bp * seq, d)
    xs = x_sample.reshape(bs, d)
    geo_p = dict(nb=bp, nt=nt, tm=TOKEN_TILE)
    geo_s = dict(nb=1, nt=1, tm=bs)

    def ffn(x, m, layer, sub, which, geo, name):
        sh, sc, gt = m
        if geo is geo_p:
            geo = dict(nb=bp, nt=seq // FFN_TILE, tm=FFN_TILE)
        return _ffn_call(x, sh, sc, gt, norm_g[layer, sub, 0][None], norm_g[layer, sub, 1][None],
                         w_ffn_in_b, w_ffn_out_b, which=(layer, which), resw=0.5, name=name, **geo)

    outs = {}
    for layer in range(DEPTH):
        (mp0, ms0), (mp1, ms1), (mp2, ms2) = mods(layer, 0), mods(layer, 1), mods(layer, 2)
        xp = ffn(xp, mp0, layer, 0, 0, geo_p, "ffn_prompt")
        xs = ffn(xs, ms0, layer, 0, 0, geo_s, "ffn_sample")
        gpre, gpost = norm_g[layer, 1, 0][None], norm_g[layer, 1, 1][None]
        if layer % 2 == 0:
            e = layer // 2
            lam_init = _lambda_init(layer)
            cw, cb = conv_a_w[e], conv_a_b[e][None]
            lng, lnb = ln_a_g[e][None], ln_a_b[e][None]
            a, q, kb, vt, k, v, st = _even_front_call(xp, mp1[0], mp1[1], gpre, w_in_even_b[e], cw, cb,
                                                      lng, lnb, **geo_p)
            hist = state_conv_a[e]
            a_s, q_s, k_s, v_s, u_s = _even_front_sample_call(
                xs, ms1[0], ms1[1], gpre, w_in_even_b[e], jnp.swapaxes(hist, 0, 1), cw, cb, lng, lnb)
            o, o_s = _attn_call(q, kb, vt, lam_qk[e], subln_g[e][None], page_table, q_s, k_s, v_s,
                                cache_k.reshape(-1, PAGE_SIZE * H_B, DV_B),
                                cache_v.reshape(-1, PAGE_SIZE * H_B, DV_B),
                                nb=bp, seq=seq, lam_init=lam_init, page_base=e * cache_k.shape[1])
            xp = _mixer_back_call(a, o, xp, mp1[2], gpost, w_out_even_b[e], name="even_back_prompt", **geo_p)
            outs["k_prompt"] = k.reshape(bp, seq, H_B, 2 * DH_B)
            outs["v_prompt"] = v.reshape(bp, seq, H_B, DV_B)
            outs["conv_a_prompt"] = st[:, HALO_A - (CONV_A - 1):, :]
            xs = _mixer_back_call(a_s, o_s.reshape(bs, D_BO), xs, ms1[2], gpost, w_out_even_b[e],
                                  name="even_back_sample", **geo_s)
            outs["k_sample"] = k_s.reshape(bs, 1, H_B, 2 * DH_B)
            outs["v_sample"] = v_s.reshape(bs, 1, H_B, DV_B)
            outs["conv_a_sample"] = jnp.concatenate([hist[:, 1:], u_s[:, None, :]], axis=1)
        else:
            od = layer // 2
            xp, pst, dst = _odd_call(xp, mp1[0], mp1[1], mp1[2], gpre, gpost, w_in_odd_b[od], w_out_odd_b[od],
                                     w_pool_b[od], pool_scale[od][None], conv_d_w[od], start_pos=0, **geo_p)
            outs["pool_prompt"] = pst[:, HALO_P - POOL_HIST:, :]
            outs["conv_d_prompt"] = dst[:, HALO_D - (CONV_D - 1):, :]
            phist, dhist = state_pool[od], state_conv_d[od]
            xs, u_s, z_s = _odd_sample_call(xs, ms1[0], ms1[1], ms1[2], gpre, gpost, w_in_odd_b[od],
                                            w_out_odd_b[od], w_pool_b[od], pool_scale[od][None], conv_d_w[od],
                                            jnp.swapaxes(phist, 0, 1), jnp.swapaxes(dhist, 0, 1),
                                            start_pos=past_len)
            outs["pool_sample"] = jnp.concatenate([phist[:, 1:], u_s[:, None, :]], axis=1)
            outs["conv_d_sample"] = jnp.concatenate([dhist[:, 1:], z_s[:, None, :]], axis=1)
        xp = ffn(xp, mp2, layer, 2, 1, geo_p, "ffn_prompt")
        xs = ffn(xs, ms2, layer, 2, 1, geo_s, "ffn_sample")

    stack = lambda name: outs[name][None]
    return (xp.reshape(bp, seq, d), xs.reshape(bs, 1, d),
            stack("k_prompt"), stack("v_prompt"), stack("k_sample"), stack("v_sample"),
            stack("conv_a_prompt"), stack("conv_a_sample"),
            stack("pool_prompt"), stack("pool_sample"),
            stack("conv_d_prompt"), stack("conv_d_sample"))
```

```python
import functools
import math

import jax
import jax.numpy as jnp
from jax import lax
from jax.experimental import pallas as pl
from jax.experimental.pallas import tpu as pltpu

F32 = jnp.float32
BF16 = jnp.bfloat16

D_MODEL = 1024
DEPTH = 2
PAGE_SIZE = 128
D_A = D_MODEL // 2
CONV_A = 31
DH_B = 64
DV_B = 2 * DH_B
H_B = (D_MODEL // 2) // DV_B
D_QK = H_B * 2 * DH_B
D_BO = H_B * DV_B
D_C = D_MODEL // 2
POOL_WINDOWS = (2, 4, 8, 16)
GC = D_C // len(POOL_WINDOWS)
POOL_HIST = max(POOL_WINDOWS) - 1
D_D = D_MODEL // 2
CONV_D = 3
D_FF = ((8 * D_MODEL // 3 + 255) // 256) * 256
N_SUB = 3
EPS = 1e-6

SUBLANES = 8
LANES = 128
VMEM_LIMIT_BYTES = 56 * 1024 * 1024

TOKEN_TILE = 512
FFN_TILE = 1024
FFN_ROW_GROUP = 256
FFN_SIDE_PARTS = 2
FF_CHUNK = 256
CONV_ROWS = 128
HALO_A = 32
HALO_P = 16
HALO_D = 8
QK_SCALE_LOG2 = (DH_B ** -0.5) * math.log2(math.e)
ATT_TQ = 512
ATT_TK = 512
ONES_ROWS = 16
PAGES_PER_CHUNK = 8
DECODE_SLOTS = 4


def _lambda_init(layer):
    return 0.8 - 0.6 * math.exp(-0.3 * layer)


def _params(semantics):
    return pltpu.CompilerParams(dimension_semantics=semantics, vmem_limit_bytes=VMEM_LIMIT_BYTES)


def _rms(x, g):
    return x * lax.rsqrt(jnp.mean(x * x, axis=-1, keepdims=True) + EPS) * g


def _mod(ref):
    v = ref[...]
    return v[0] if v.ndim == 3 else v


def _unit_rms(x):
    return x * lax.rsqrt(jnp.mean(x * x, axis=-1, keepdims=True) + EPS)


def _pre(x, g, scale, shift):
    return _unit_rms(x) * (g * (1.0 + scale)) + shift


def _post(x, o, g, gate, resw=1.0):
    return x + _unit_rms(o) * (g * (resw * (1.0 + gate)))


def _silu(x):
    return x * jax.nn.sigmoid(x)


def _dot(a, b):
    return jnp.dot(a, b, preferred_element_type=F32)


def _resident(shape, lead=()):
    block = (None,) * len(lead) + tuple(shape)[len(lead):]
    index = tuple(lead) + (0,) * (len(shape) - len(lead))
    return pl.BlockSpec(block, lambda *_: index, pipeline_mode=pl.Buffered(1))


def _mod_spec(arr, tm):
    if arr.ndim == 3:
        return pl.BlockSpec((1, 1, arr.shape[-1]), lambda b, t: (b, 0, 0))
    return pl.BlockSpec((tm, arr.shape[-1]), lambda b, t: (0, 0))


def _ada_body(c_ref, w_ref, b_ref, o_ref):
    c = c_ref[...]
    o_ref[0] = _dot(_silu(c).astype(BF16), w_ref[0].astype(BF16)) + b_ref[0]


def _ada_call(c_all, w_ada, b_ada):
    rows, d = c_all.shape
    depth, _, n = w_ada.shape
    tn = 1024
    return pl.pallas_call(
        _ada_body,
        out_shape=jax.ShapeDtypeStruct((depth, rows, n), F32),
        grid=(depth, n // tn),
        in_specs=[pl.BlockSpec((rows, d), lambda i, j: (0, 0)),
                  pl.BlockSpec((1, d, tn), lambda i, j: (i, 0, j)),
                  pl.BlockSpec((1, 1, tn), lambda i, j: (i, 0, j))],
        out_specs=pl.BlockSpec((1, rows, tn), lambda i, j: (i, 0, j)),
        compiler_params=_params(("parallel", "parallel")),
        name="ada_mod",
    )(c_all, w_ada, b_ada.reshape(depth, 1, n))


def _ffn_body(*refs, resw, mix):
    if mix:
        (x_ref, sh_ref, sc_ref, gt_ref, gpre_ref, gpost_ref, win_ref, wout_ref,
         a_ref, b_ref, mgt_ref, mg_ref, wmix_ref, o_ref, h_ref, act_ref, y_ref) = refs
    else:
        (x_ref, sh_ref, sc_ref, gt_ref, gpre_ref, gpost_ref, win_ref, wout_ref,
         o_ref, h_ref, act_ref, y_ref) = refs
    rows = x_ref.shape[0]
    group = min(rows, FFN_ROW_GROUP)
    n_groups = rows // group
    parts = FFN_SIDE_PARTS if group % (FFN_SIDE_PARTS * SUBLANES) == 0 and not mix else 1
    sub = group // parts

    def mod(ref, rs):
        v = _mod(ref)
        return v if v.shape[0] == 1 else v[rs]

    def pre_part(g, p):
        rs = slice(g * group + p * sub, g * group + (p + 1) * sub)
        x = x_ref[rs, :]
        if mix:
            half = a_ref.shape[-1]
            r = _dot(a_ref[rs, :], wmix_ref[0:half, :]) + _dot(b_ref[rs, :], wmix_ref[half:2 * half, :])
            x = _post(x, r, mg_ref[...], mod(mgt_ref, rs))
            o_ref[rs, :] = x
        h_ref[rs, :] = _pre(x, gpre_ref[...], mod(sc_ref, rs), mod(sh_ref, rs)).astype(BF16)

    def post_part(g, p):
        rs = slice(g * group + p * sub, g * group + (p + 1) * sub)
        x = o_ref[rs, :] if mix else x_ref[rs, :]
        o_ref[rs, :] = _post(x, y_ref[rs, :], gpost_ref[...], mod(gt_ref, rs), resw)

    def matmul_steps(g):
        rs = slice(g * group, (g + 1) * group)

        def up(j):
            lo = j * FF_CHUNK
            h = h_ref[rs, :]
            gate = _dot(h, win_ref[:, lo:lo + FF_CHUNK])
            val = _dot(h, win_ref[:, D_FF + lo:D_FF + lo + FF_CHUNK])
            act_ref[rs, lo:lo + FF_CHUNK] = (_silu(gate) * val).astype(BF16)

        def down():
            y_ref[rs, :] = _dot(act_ref[rs, :], wout_ref[...])

        return [functools.partial(up, j) for j in range(D_FF // FF_CHUNK)] + [down]

    for p in range(parts):
        pre_part(0, p)
    for g in range(n_groups):
        side = [functools.partial(pre_part, g + 1, p) for p in range(parts)] if g + 1 < n_groups else []
        side += [functools.partial(post_part, g - 1, p) for p in range(parts)] if g > 0 else []
        main = matmul_steps(g)
        gap = max(1, len(main) // (len(side) + 1))
        for i, step in enumerate(main):
            step()
            if side and (i + 1) % gap == 0:
                side.pop(0)()
        for rest in side:
            rest()
    for p in range(parts):
        post_part(n_groups - 1, p)


def _ffn_call(x, sh, sc, gt, gpre, gpost, w_in, w_out, *, which, resw, nb, nt, tm, name, mix=None):
    n, d = x.shape
    row = lambda w: pl.BlockSpec((tm, w), lambda b, t: (b * nt + t, 0))
    args = [x, sh, sc, gt, gpre, gpost, w_in, w_out]
    in_specs = [row(d), _mod_spec(sh, tm), _mod_spec(sc, tm), _mod_spec(gt, tm),
                _resident((1, d)), _resident((1, d)),
                _resident(w_in.shape, which), _resident(w_out.shape, which)]
    if mix:
        a, b, mgt, mg, w_mix = mix
        args += [a, b, mgt, mg, w_mix]
        in_specs += [row(a.shape[-1]), row(b.shape[-1]), _mod_spec(mgt, tm), _resident((1, d)),
                     _resident(w_mix.shape)]
    return pl.pallas_call(
        functools.partial(_ffn_body, resw=resw, mix=bool(mix)),
        out_shape=jax.ShapeDtypeStruct((n, d), F32),
        grid=(nb, nt),
        in_specs=in_specs,
        out_specs=row(d),
        scratch_shapes=[pltpu.VMEM((tm, d), BF16), pltpu.VMEM((tm, D_FF), BF16), pltpu.VMEM((tm, d), F32)],
        compiler_params=_params(("parallel", "parallel")),
        name=name,
    )(*args)


def _even_front_body(x_ref, sh_ref, sc_ref, gpre_ref, win_ref, cw_ref, cb_ref, lng_ref, lnb_ref,
                     a_ref, qt_ref, kb_ref, vt_ref, k_ref, v_ref, st_ref, ubuf, *, tm):
    @pl.when(pl.program_id(1) == 0)
    def _():
        ubuf[0:HALO_A, :] = jnp.zeros((HALO_A, D_A), F32)

    h = _pre(x_ref[...], gpre_ref[...], _mod(sc_ref), _mod(sh_ref)).astype(BF16)

    def proj(c):
        return _dot(h, win_ref[:, c * D_A:(c + 1) * D_A])

    ubuf[HALO_A:HALO_A + tm, :] = proj(0) * jax.nn.sigmoid(proj(1))
    qt_ref[...] = (proj(2) * QK_SCALE_LOG2).T.astype(BF16)
    k = proj(3)
    kb_ref[...] = k.astype(BF16)
    v = proj(4)
    vt_ref[...] = v.T.astype(BF16)
    for hd in range(H_B):
        k_ref[pl.ds(hd, tm, stride=H_B), :] = k[:, hd * DV_B:(hd + 1) * DV_B]
        v_ref[pl.ds(hd, tm, stride=H_B), :] = v[:, hd * DV_B:(hd + 1) * DV_B]

    first = HALO_A - (CONV_A - 1)
    for c in range(tm // CONV_ROWS):
        base = c * CONV_ROWS
        acc = jnp.broadcast_to(cb_ref[...], (CONV_ROWS, D_A))
        for res in range(SUBLANES):
            span = CONV_ROWS + (SUBLANES if res else 0)
            part = None
            for blk in range((first + CONV_A - 1) // SUBLANES + 1):
                j = blk * SUBLANES + res - first
                if 0 <= j < CONV_A:
                    lo = base + blk * SUBLANES
                    term = cw_ref[j:j + 1, :] * ubuf[lo:lo + span, :]
                    part = term if part is None else part + term
            acc = acc + part[res:res + CONV_ROWS, :]
        mu = jnp.mean(acc, axis=-1, keepdims=True)
        cen = acc - mu
        var = jnp.mean(cen * cen, axis=-1, keepdims=True)
        y = cen * lax.rsqrt(var + EPS) * lng_ref[...] + lnb_ref[...]
        a_ref[base:base + CONV_ROWS, :] = _silu(y).astype(BF16)

    tail = ubuf[tm:tm + HALO_A, :]
    st_ref[0] = tail
    ubuf[0:HALO_A, :] = tail


def _even_front_call(x, sh, sc, gpre, w_in, cw, cb, lng, lnb, *, nb, nt, tm):
    n, d = x.shape
    row = lambda w: pl.BlockSpec((tm, w), lambda b, t: (b * nt + t, 0))
    out_shape = (jax.ShapeDtypeStruct((n, D_A), BF16),
                 jax.ShapeDtypeStruct((D_QK, n), BF16),
                 jax.ShapeDtypeStruct((n, D_QK), BF16),
                 jax.ShapeDtypeStruct((D_BO, n), BF16),
                 jax.ShapeDtypeStruct((n * H_B, DV_B), F32),
                 jax.ShapeDtypeStruct((n * H_B, DV_B), F32),
                 jax.ShapeDtypeStruct((nb, HALO_A, D_A), F32))
    return pl.pallas_call(
        functools.partial(_even_front_body, tm=tm),
        out_shape=out_shape,
        grid=(nb, nt),
        in_specs=[row(d), _mod_spec(sh, tm), _mod_spec(sc, tm), _resident((1, d)),
                  _resident(w_in.shape), _resident(cw.shape), _resident((1, D_A)),
                  _resident((1, D_A)), _resident((1, D_A))],
        out_specs=(row(D_A),
                   pl.BlockSpec((D_QK, tm), lambda b, t: (0, b * nt + t)),
                   row(D_QK),
                   pl.BlockSpec((D_BO, tm), lambda b, t: (0, b * nt + t)),
                   pl.BlockSpec((tm * H_B, DV_B), lambda b, t: (b * nt + t, 0)),
                   pl.BlockSpec((tm * H_B, DV_B), lambda b, t: (b * nt + t, 0)),
                   pl.BlockSpec((1, HALO_A, D_A), lambda b, t: (b, 0, 0))),
        scratch_shapes=[pltpu.VMEM((HALO_A + tm, D_A), F32)],
        compiler_params=_params(("arbitrary", "arbitrary")),
        name="even_front_prompt",
    )(x, sh, sc, gpre, w_in, cw, cb, lng, lnb)


def _diff_lambda(lamqk_ref, lam_init):
    lq = lamqk_ref[...]
    s1 = jnp.sum(lq[0:1] * lq[1:2], axis=1, keepdims=True)
    s2 = jnp.sum(lq[2:3] * lq[3:4], axis=1, keepdims=True)
    return jnp.exp(s1) - jnp.exp(s2) + lam_init


def _prompt_attn_tile(qi, q_ref, k_ref, vx_ref, lam, g_ref, o_ref, qt_ref, sa_ref, sb_ref, m_ref, acc_ref,
                      *, lam_init, tq, tk):
    q = q_ref[...]
    dim = lax.broadcasted_iota(jnp.int32, q.shape, 0)
    zero = jnp.zeros_like(q)
    qt_ref[:, 0:tq] = jnp.where(dim < DH_B, q, zero)
    qt_ref[:, tq:2 * tq] = jnp.where(dim >= DH_B, q, zero)

    m_ref[...] = jnp.full(m_ref.shape, -jnp.inf, F32)
    acc_ref[...] = jnp.zeros(acc_ref.shape, F32)

    def key_start(kt):
        return pl.multiple_of(kt * tk, tk)

    def scores(kt):
        return _dot(k_ref[pl.ds(key_start(kt), tk), :], qt_ref[...])

    def consume(s_ref, kt, masked):
        start = key_start(kt)
        for half in range(2):
            cols = slice(half * tq, (half + 1) * tq)
            s = s_ref[:, cols]
            if masked:
                kpos = start + lax.broadcasted_iota(jnp.int32, s.shape, 0)
                qpos = qi * tq + lax.broadcasted_iota(jnp.int32, s.shape, 1)
                s = jnp.where(kpos <= qpos, s, -jnp.inf)
            m_old = m_ref[:, cols]
            m_new = jnp.maximum(m_old, jnp.max(s, axis=0, keepdims=True))
            alpha = jnp.exp2(m_old - m_new)
            p = jnp.exp2(s - m_new).astype(BF16)
            pv = _dot(vx_ref[:, pl.ds(start, tk)], p)
            acc_ref[:, cols] = alpha * acc_ref[:, cols] + pv
            m_ref[:, cols] = m_new

    n_full = (qi * tq) // tk
    sa_ref[...] = scores(0)

    def pair(j, carry):
        sb_ref[...] = scores(2 * j + 1)
        consume(sa_ref, 2 * j, masked=False)
        sa_ref[...] = scores(2 * j + 2)
        consume(sb_ref, 2 * j + 1, masked=False)
        return carry

    lax.fori_loop(0, n_full // 2, pair, 0)

    @pl.when(n_full % 2 == 1)
    def _():
        sb_ref[...] = scores(n_full)
        consume(sa_ref, n_full - 1, masked=False)
        consume(sb_ref, n_full, masked=True)

    @pl.when(n_full % 2 == 0)
    def _():
        consume(sa_ref, n_full, masked=True)

    nt = acc_ref[0:DV_B, :] / acc_ref[DV_B:DV_B + 1, :]
    ot = nt[:, 0:tq] - lam * nt[:, tq:2 * tq]
    ot = ot * lax.rsqrt(jnp.mean(ot * ot, axis=0, keepdims=True) + EPS)
    o_ref[...] = (ot.T * g_ref[...] * (1.0 - lam_init)).astype(o_ref.dtype)


def _div_rem(x, d):
    if isinstance(x, int):
        return divmod(x, d)
    return lax.div(x, d), lax.rem(x, d)


def _decode_chunk(gc, pt_ref, q_ref, kn_ref, vn_ref, lam, g_ref, ck_hbm, cv_hbm, o_ref, kbuf, vbuf, sem, st_ref,
                  *, lam_init, n_seq, n_pages, pc, page_base):
    nc = n_pages // pc
    total = n_seq * nc
    ahead = DECODE_SLOTS - 1
    page_rows = PAGE_SIZE * H_B
    n_comp = 2 * H_B

    def chunk_copies(idx):
        seq, c = _div_rem(idx, nc)
        slot = _div_rem(idx, DECODE_SLOTS)[1]
        copies = []
        for i in range(pc):
            page = page_base + pt_ref[seq * n_pages + c * pc + i]
            dst = pl.ds(i * page_rows, page_rows)
            copies.append(pltpu.make_async_copy(ck_hbm.at[page], kbuf.at[slot, dst], sem.at[0, slot]))
            copies.append(pltpu.make_async_copy(cv_hbm.at[page], vbuf.at[slot, dst], sem.at[1, slot]))
        return copies

    @pl.when(gc == 0)
    def _():
        for idx in range(ahead):
            for cp in chunk_copies(idx):
                cp.start()

    seq, c = _div_rem(gc, nc)
    slot = lax.rem(gc, DECODE_SLOTS)

    @pl.when(c == 0)
    def _():
        st_ref[0] = jnp.full((n_comp, DV_B), -jnp.inf, F32)
        st_ref[1] = jnp.zeros((n_comp, DV_B), F32)
        st_ref[2] = jnp.zeros((n_comp, DV_B), F32)

    for cp in chunk_copies(gc):
        cp.wait()

    @pl.when(gc + ahead < total)
    def _():
        for cp in chunk_copies(gc + ahead):
            cp.start()

    lane = lax.broadcasted_iota(jnp.int32, (1, DV_B), 1)
    q_row = q_ref[pl.ds(seq, 1), :]
    q_rows = []
    for h in range(H_B):
        qh = q_row[:, h * DV_B:(h + 1) * DV_B]
        q_rows += [jnp.where(lane < DH_B, qh, 0.0), jnp.where(lane >= DH_B, qh, 0.0)]
    q8 = jnp.concatenate(q_rows, axis=0)

    m, l, acc = st_ref[0][:, 0:1], st_ref[1][:, 0:1], st_ref[2]
    half = kbuf.shape[1] // 2
    zeros = jnp.zeros_like(q8)
    q16 = jnp.concatenate([jnp.concatenate([q8, zeros], axis=1),
                           jnp.concatenate([zeros, q8], axis=1)], axis=0)
    kc = jnp.concatenate([kbuf[slot, 0:half, :], kbuf[slot, half:2 * half, :]], axis=1)
    s = lax.dot_general(q16.astype(BF16), kc.astype(BF16), (((1,), (1,)), ((), ())),
                        preferred_element_type=F32)
    row_head = lax.rem(lax.broadcasted_iota(jnp.int32, s.shape, 1), H_B)
    own_head = lax.rem(lax.broadcasted_iota(jnp.int32, s.shape, 0), n_comp) // 2
    s = jnp.where(row_head == own_head, s, -jnp.inf)
    s_max = jnp.max(s, axis=-1, keepdims=True)
    m_new = jnp.maximum(m, jnp.maximum(s_max[0:n_comp], s_max[n_comp:2 * n_comp]))
    alpha = jnp.exp(m - m_new)
    p = jnp.exp(s - jnp.concatenate([m_new, m_new], axis=0))
    p_sum = jnp.sum(p, axis=-1, keepdims=True)
    l = alpha * l + p_sum[0:n_comp] + p_sum[n_comp:2 * n_comp]
    vc = jnp.concatenate([vbuf[slot, 0:half, :], vbuf[slot, half:2 * half, :]], axis=1)
    pv = _dot(p.astype(BF16), vc.astype(BF16))
    acc = alpha * acc + pv[0:n_comp, 0:DV_B] + pv[n_comp:2 * n_comp, DV_B:2 * DV_B]
    st_ref[0] = jnp.broadcast_to(m_new, (n_comp, DV_B))
    st_ref[1] = jnp.broadcast_to(l, (n_comp, DV_B))
    st_ref[2] = acc

    @pl.when(c == nc - 1)
    def _():
        k_row, v_row = kn_ref[pl.ds(seq, 1), :], vn_ref[pl.ds(seq, 1), :]
        k8 = jnp.concatenate([k_row[:, (r // 2) * DV_B:(r // 2 + 1) * DV_B] for r in range(n_comp)], axis=0)
        v8 = jnp.concatenate([v_row[:, (r // 2) * DV_B:(r // 2 + 1) * DV_B] for r in range(n_comp)], axis=0)
        s_new = jnp.sum(q8 * k8, axis=-1, keepdims=True)
        m_fin = jnp.maximum(m_new, s_new)
        a_fin = jnp.exp(m_new - m_fin)
        p_new = jnp.exp(s_new - m_fin)
        n = (a_fin * acc + p_new * v8) / (a_fin * l + p_new)
        o = jnp.concatenate([n[2 * h:2 * h + 1] - lam * n[2 * h + 1:2 * h + 2] for h in range(H_B)], axis=0)
        o_ref[seq] = _rms(o, g_ref[...]) * (1.0 - lam_init)


def _attn_body(pt_ref, q_ref, k_ref, vt_ref, lamqk_ref, g_ref, qs_ref, ks_ref, vs_ref, ck_hbm, cv_hbm,
               o_ref, os_ref, qt_ref, sa_ref, sb_ref, m_ref, acc_ref, vx_ref, kbuf, vbuf, sem, st_ref,
               *, lam_init, tq, tk, chunks_per_step, decode):
    qi = pl.program_id(2)
    step = (pl.program_id(0) * pl.num_programs(1) + pl.program_id(1)) * pl.num_programs(2) + qi
    lam = _diff_lambda(lamqk_ref, lam_init)
    for j in range(chunks_per_step):
        _decode_chunk(step * chunks_per_step + j, pt_ref, qs_ref, ks_ref, vs_ref, lam, g_ref, ck_hbm, cv_hbm,
                      os_ref, kbuf, vbuf, sem, st_ref, lam_init=lam_init, **decode)

    @pl.when(qi == 0)
    def _():
        vx_ref[0:DV_B, :] = vt_ref[...]
        vx_ref[DV_B:DV_B + ONES_ROWS, :] = jnp.ones((ONES_ROWS, vx_ref.shape[1]), BF16)

    _prompt_attn_tile(qi, q_ref, k_ref, vx_ref, lam, g_ref, o_ref, qt_ref, sa_ref, sb_ref,
                      m_ref, acc_ref, lam_init=lam_init, tq=tq, tk=tk)


def _attn_call(qt, k, vt, lam_qk, g, page_table, q_s, k_s, v_s, cache_k, cache_v, *, nb, seq, lam_init, page_base):
    n = k.shape[0]
    nq = seq // ATT_TQ
    n_seq, n_pages = page_table.shape
    pc = PAGES_PER_CHUNK
    n_steps = nb * H_B * nq
    total_chunks = n_seq * (n_pages // pc)
    assert n_pages % pc == 0 and total_chunks % n_steps == 0 and DECODE_SLOTS - 1 <= total_chunks
    rows = pc * PAGE_SIZE * H_B
    full = lambda shape: pl.BlockSpec(shape, lambda b, h, i, pt: (0,) * len(shape))
    grid_spec = pltpu.PrefetchScalarGridSpec(
        num_scalar_prefetch=1,
        grid=(nb, H_B, nq),
        in_specs=[pl.BlockSpec((DV_B, ATT_TQ), lambda b, h, i, pt: (h, b * nq + i)),
                  pl.BlockSpec((seq, DV_B), lambda b, h, i, pt: (b, h)),
                  pl.BlockSpec((DV_B, seq), lambda b, h, i, pt: (h, b)),
                  full(lam_qk.shape), full((1, DV_B)),
                  full(q_s.shape), full(k_s.shape), full(v_s.shape),
                  pl.BlockSpec(memory_space=pl.ANY), pl.BlockSpec(memory_space=pl.ANY)],
        out_specs=(pl.BlockSpec((ATT_TQ, DV_B), lambda b, h, i, pt: (b * nq + i, h)),
                   full((n_seq, H_B, DV_B))),
        scratch_shapes=[pltpu.VMEM((DV_B, 2 * ATT_TQ), BF16),
                        pltpu.VMEM((ATT_TK, 2 * ATT_TQ), F32),
                        pltpu.VMEM((ATT_TK, 2 * ATT_TQ), F32),
                        pltpu.VMEM((1, 2 * ATT_TQ), F32),
                        pltpu.VMEM((DV_B + ONES_ROWS, 2 * ATT_TQ), F32),
                        pltpu.VMEM((DV_B + ONES_ROWS, seq), BF16),
                        pltpu.VMEM((DECODE_SLOTS, rows, DV_B), F32),
                        pltpu.VMEM((DECODE_SLOTS, rows, DV_B), F32),
                        pltpu.SemaphoreType.DMA((2, DECODE_SLOTS)),
                        pltpu.VMEM((3, 2 * H_B, DV_B), F32)])
    decode = dict(n_seq=n_seq, n_pages=n_pages, pc=pc, page_base=page_base)
    return pl.pallas_call(
        functools.partial(_attn_body, lam_init=lam_init, tq=ATT_TQ, tk=ATT_TK,
                          chunks_per_step=total_chunks // n_steps, decode=decode),
        out_shape=(jax.ShapeDtypeStruct((n, D_BO), BF16),
                   jax.ShapeDtypeStruct((n_seq, H_B, DV_B), F32)),
        grid_spec=grid_spec,
        compiler_params=_params(("arbitrary", "arbitrary", "arbitrary")),
        name="diff_attn",
    )(page_table.reshape(-1), qt, k, vt, lam_qk, g, q_s, k_s, v_s, cache_k, cache_v)


def _mixer_back_body(a_ref, b_ref, x_ref, gt_ref, gpost_ref, w_ref, o_ref):
    half = a_ref.shape[-1]
    r = (_dot(a_ref[...].astype(BF16), w_ref[0:half, :])
         + _dot(b_ref[...].astype(BF16), w_ref[half:2 * half, :]))
    o_ref[...] = _post(x_ref[...], r, gpost_ref[...], _mod(gt_ref))


def _mixer_back_call(a, b, x, gt, gpost, w_out, *, nb, nt, tm, name):
    n, d = x.shape
    row = lambda w: pl.BlockSpec((tm, w), lambda bb, t: (bb * nt + t, 0))
    return pl.pallas_call(
        _mixer_back_body,
        out_shape=jax.ShapeDtypeStruct((n, d), F32),
        grid=(nb, nt),
        in_specs=[row(a.shape[-1]), row(b.shape[-1]), row(d), _mod_spec(gt, tm),
                  _resident((1, d)), _resident(w_out.shape)],
        out_specs=row(d),
        compiler_params=_params(("parallel", "parallel")),
        name=name,
    )(a, b, x, gt, gpost, w_out)


def _odd_body(x_ref, sh_ref, sc_ref, gt_ref, gpre_ref, gpost_ref, win_ref, wout_ref, wpool_ref,
              pscale_ref, cw_ref, o_ref, pst_ref, dst_ref, pbuf, zbuf, mix_ref, *, tm, start_pos):
    t = pl.program_id(1)

    @pl.when(t == 0)
    def _():
        pbuf[0:HALO_P, :] = jnp.zeros((HALO_P, D_C), F32)
        zbuf[0:HALO_D, :] = jnp.zeros((HALO_D, D_D), F32)

    x = x_ref[...]
    h = _pre(x, gpre_ref[...], _mod(sc_ref), _mod(sh_ref)).astype(BF16)

    def proj(c):
        return _dot(h, win_ref[:, c * D_C:(c + 1) * D_C])

    u = proj(0)
    pbuf[HALO_P:HALO_P + tm, :] = u
    bg = proj(1)
    zbuf[HALO_D:HALO_D + tm, :] = proj(2) * proj(3)

    pos = start_pos + t * tm + lax.broadcasted_iota(jnp.int32, (tm, 1), 0)
    for g, w in enumerate(POOL_WINDOWS):
        cols = slice(g * GC, (g + 1) * GC)
        win = u[:, cols]
        for dlt in range(1, w):
            win = win + pbuf[HALO_P - dlt:HALO_P - dlt + tm, cols]
        cnt = jnp.minimum(pos + 1, w).astype(F32)
        mdiff = (win / cnt - u[:, cols]).astype(BF16)
        mix_ref[:, cols] = (_dot(mdiff, wpool_ref[g]) * pscale_ref[:, cols]).astype(BF16)

    first = HALO_D - (CONV_D - 1)
    conv = cw_ref[0:1, :] * zbuf[first:first + tm, :]
    for j in range(1, CONV_D):
        conv = conv + cw_ref[j:j + 1, :] * zbuf[first + j:first + j + tm, :]
    mix_ref[:, D_C:D_C + D_D] = (bg * conv).astype(BF16)

    r = _dot(mix_ref[...], wout_ref[...])
    o_ref[...] = _post(x, r, gpost_ref[...], _mod(gt_ref))

    ptail = pbuf[tm:tm + HALO_P, :]
    pst_ref[0] = ptail
    pbuf[0:HALO_P, :] = ptail
    ztail = zbuf[tm:tm + HALO_D, :]
    dst_ref[0] = ztail
    zbuf[0:HALO_D, :] = ztail


def _odd_call(x, sh, sc, gt, gpre, gpost, w_in, w_out, w_pool, pscale, cw, *, nb, nt, tm, start_pos):
    n, d = x.shape
    row = pl.BlockSpec((tm, d), lambda b, t: (b * nt + t, 0))
    out_shape = (jax.ShapeDtypeStruct((n, d), F32),
                 jax.ShapeDtypeStruct((nb, HALO_P, D_C), F32),
                 jax.ShapeDtypeStruct((nb, HALO_D, D_D), F32))
    return pl.pallas_call(
        functools.partial(_odd_body, tm=tm, start_pos=start_pos),
        out_shape=out_shape,
        grid=(nb, nt),
        in_specs=[row, _mod_spec(sh, tm), _mod_spec(sc, tm), _mod_spec(gt, tm),
                  _resident((1, d)), _resident((1, d)), _resident(w_in.shape), _resident(w_out.shape),
                  _resident(w_pool.shape), _resident((1, D_C)), _resident(cw.shape)],
        out_specs=(row,
                   pl.BlockSpec((1, HALO_P, D_C), lambda b, t: (b, 0, 0)),
                   pl.BlockSpec((1, HALO_D, D_D), lambda b, t: (b, 0, 0))),
        scratch_shapes=[pltpu.VMEM((HALO_P + tm, D_C), F32),
                        pltpu.VMEM((HALO_D + tm, D_D), F32),
                        pltpu.VMEM((tm, D_C + D_D), BF16)],
        compiler_params=_params(("arbitrary", "arbitrary")),
        name="odd_mixer_prompt",
    )(x, sh, sc, gt, gpre, gpost, w_in, w_out, w_pool, pscale, cw)


def _even_front_sample_body(x_ref, sh_ref, sc_ref, gpre_ref, win_ref, hist_ref, cw_ref, cb_ref,
                            lng_ref, lnb_ref, a_ref, q_ref, k_ref, v_ref, u_ref):
    x = x_ref[...]
    h = _pre(x, gpre_ref[...], sc_ref[...], sh_ref[...]).astype(BF16)

    def proj(c):
        return _dot(h, win_ref[:, c * D_A:(c + 1) * D_A])

    u = proj(0) * jax.nn.sigmoid(proj(1))
    u_ref[...] = u
    q_ref[...] = proj(2) * (DH_B ** -0.5)
    k_ref[...] = proj(3)
    v_ref[...] = proj(4)

    acc = cb_ref[...] + cw_ref[CONV_A - 1:CONV_A, :] * u
    for j in range(CONV_A - 1):
        acc = acc + cw_ref[j:j + 1, :] * hist_ref[j]
    mu = jnp.mean(acc, axis=-1, keepdims=True)
    cen = acc - mu
    var = jnp.mean(cen * cen, axis=-1, keepdims=True)
    y = cen * lax.rsqrt(var + EPS) * lng_ref[...] + lnb_ref[...]
    a_ref[...] = _silu(y)


def _even_front_sample_call(x, sh, sc, gpre, w_in, hist_t, cw, cb, lng, lnb):
    n, d = x.shape
    full = lambda shape: pl.BlockSpec(shape, lambda i: (0,) * len(shape))
    out = jax.ShapeDtypeStruct((n, D_A), F32)
    return pl.pallas_call(
        _even_front_sample_body,
        out_shape=(out, out, out, out, out),
        grid=(1,),
        in_specs=[full((n, d)), full((n, d)), full((n, d)), full((1, d)), full(w_in.shape),
                  full(hist_t.shape), full(cw.shape), full((1, D_A)), full((1, D_A)), full((1, D_A))],
        out_specs=tuple(full((n, D_A)) for _ in range(5)),
        compiler_params=_params(("arbitrary",)),
        name="even_front_sample",
    )(x, sh, sc, gpre, w_in, hist_t, cw, cb, lng, lnb)


def _odd_sample_body(x_ref, sh_ref, sc_ref, gt_ref, gpre_ref, gpost_ref, win_ref, wout_ref, wpool_ref,
                     pscale_ref, cw_ref, phist_ref, dhist_ref, o_ref, u_ref, z_ref, mix_ref, *, start_pos):
    x = x_ref[...]
    h = _pre(x, gpre_ref[...], sc_ref[...], sh_ref[...]).astype(BF16)

    def proj(c):
        return _dot(h, win_ref[:, c * D_C:(c + 1) * D_C])

    u = proj(0)
    u_ref[...] = u
    bg = proj(1)
    z = proj(2) * proj(3)
    z_ref[...] = z

    for g, w in enumerate(POOL_WINDOWS):
        cols = slice(g * GC, (g + 1) * GC)
        win = u[:, cols]
        for dlt in range(1, w):
            win = win + phist_ref[POOL_HIST - dlt][:, cols]
        cnt = float(min(start_pos + 1, w))
        mdiff = (win / cnt - u[:, cols]).astype(BF16)
        mix_ref[:, cols] = (_dot(mdiff, wpool_ref[g]) * pscale_ref[:, cols]).astype(BF16)

    conv = cw_ref[CONV_D - 1:CONV_D, :] * z
    for j in range(CONV_D - 1):
        conv = conv + cw_ref[j:j + 1, :] * dhist_ref[j]
    mix_ref[:, D_C:D_C + D_D] = (bg * conv).astype(BF16)

    r = _dot(mix_ref[...], wout_ref[...])
    o_ref[...] = _post(x, r, gpost_ref[...], gt_ref[...])


def _odd_sample_call(x, sh, sc, gt, gpre, gpost, w_in, w_out, w_pool, pscale, cw, phist_t, dhist_t,
                     *, start_pos):
    n, d = x.shape
    full = lambda shape: pl.BlockSpec(shape, lambda i: (0,) * len(shape))
    return pl.pallas_call(
        functools.partial(_odd_sample_body, start_pos=start_pos),
        out_shape=(jax.ShapeDtypeStruct((n, d), F32),
                   jax.ShapeDtypeStruct((n, D_C), F32),
                   jax.ShapeDtypeStruct((n, D_D), F32)),
        grid=(1,),
        in_specs=[full((n, d)), full((n, d)), full((n, d)), full((n, d)), full((1, d)), full((1, d)),
                  full(w_in.shape), full(w_out.shape), full(w_pool.shape), full((1, D_C)),
                  full(cw.shape), full(phist_t.shape), full(dhist_t.shape)],
        out_specs=(full((n, d)), full((n, D_C)), full((n, D_D))),
        scratch_shapes=[pltpu.VMEM((n, D_C + D_D), BF16)],
        compiler_params=_params(("arbitrary",)),
        name="odd_mixer_sample",
    )(x, sh, sc, gt, gpre, gpost, w_in, w_out, w_pool, pscale, cw, phist_t, dhist_t)


def kernel(x_prompt, x_sample, cache_k, cache_v, state_conv_a, state_pool, state_conv_d, page_table,
           c_prompt, c_sample, w_ada, b_ada, norm_g, w_ffn_in, w_ffn_out, w_in_even, w_out_even,
           conv_a_w, conv_a_b, ln_a_g, ln_a_b, lam_qk, subln_g, w_in_odd, w_out_odd, w_pool,
           pool_scale, conv_d_w):
    bp, seq, d = x_prompt.shape
    bs = x_sample.shape[0]
    assert x_sample.shape[1] == 1 and d == D_MODEL and seq % TOKEN_TILE == 0 and seq % FFN_TILE == 0
    past_len = page_table.shape[1] * PAGE_SIZE
    nt = seq // TOKEN_TILE

    n_c = bp + bs
    pad = (-n_c) % SUBLANES
    c_all = jnp.concatenate([c_prompt, c_sample, jnp.zeros((pad, d), F32)], axis=0)
    mod = _ada_call(c_all, w_ada, b_ada)

    def mods(layer, sub):
        out_p, out_s = [], []
        for j in range(3):
            col = (sub * 3 + j) * d
            out_p.append(mod[layer, :bp, col:col + d].reshape(bp, 1, d))
            out_s.append(mod[layer, bp:n_c, col:col + d])
        return out_p, out_s

    w_ffn_in_b = w_ffn_in.astype(BF16)
    w_ffn_out_b = w_ffn_out.astype(BF16)
    w_in_even_b = w_in_even.astype(BF16)
    w_out_even_b = w_out_even.astype(BF16)
    w_in_odd_b = w_in_odd.astype(BF16)
    w_out_odd_b = w_out_odd.astype(BF16)
    w_pool_b = w_pool.astype(BF16)

    xp = x_prompt.reshape(bp * seq, d)
    xs = x_sample.reshape(bs, d)
    geo_p = dict(nb=bp, nt=nt, tm=TOKEN_TILE)
    geo_s = dict(nb=1, nt=1, tm=bs)

    def ffn(x, m, layer, sub, which, geo, name, mix=None):
        sh, sc, gt = m
        if geo is geo_p:
            geo = dict(nb=bp, nt=seq // FFN_TILE, tm=FFN_TILE)
        return _ffn_call(x, sh, sc, gt, norm_g[layer, sub, 0][None], norm_g[layer, sub, 1][None],
                         w_ffn_in_b, w_ffn_out_b, which=(layer, which), resw=0.5, name=name, mix=mix, **geo)

    outs = {}
    for layer in range(DEPTH):
        (mp0, ms0), (mp1, ms1), (mp2, ms2) = mods(layer, 0), mods(layer, 1), mods(layer, 2)
        xp = ffn(xp, mp0, layer, 0, 0, geo_p, "ffn_prompt")
        xs = ffn(xs, ms0, layer, 0, 0, geo_s, "ffn_sample")
        gpre, gpost = norm_g[layer, 1, 0][None], norm_g[layer, 1, 1][None]
        if layer % 2 == 0:
            e = layer // 2
            lam_init = _lambda_init(layer)
            cw, cb = conv_a_w[e], conv_a_b[e][None]
            lng, lnb = ln_a_g[e][None], ln_a_b[e][None]
            a, q, kb, vt, k, v, st = _even_front_call(xp, mp1[0], mp1[1], gpre, w_in_even_b[e], cw, cb,
                                                      lng, lnb, **geo_p)
            hist = state_conv_a[e]
            a_s, q_s, k_s, v_s, u_s = _even_front_sample_call(
                xs, ms1[0], ms1[1], gpre, w_in_even_b[e], jnp.swapaxes(hist, 0, 1), cw, cb, lng, lnb)
            o, o_s = _attn_call(q, kb, vt, lam_qk[e], subln_g[e][None], page_table, q_s, k_s, v_s,
                                cache_k.reshape(-1, PAGE_SIZE * H_B, DV_B),
                                cache_v.reshape(-1, PAGE_SIZE * H_B, DV_B),
                                nb=bp, seq=seq, lam_init=lam_init, page_base=e * cache_k.shape[1])
            prompt_mix = (a, o, mp1[2], gpost, w_out_even_b[e])
            outs["k_prompt"] = k.reshape(bp, seq, H_B, 2 * DH_B)
            outs["v_prompt"] = v.reshape(bp, seq, H_B, DV_B)
            outs["conv_a_prompt"] = st[:, HALO_A - (CONV_A - 1):, :]
            xs = _mixer_back_call(a_s, o_s.reshape(bs, D_BO), xs, ms1[2], gpost, w_out_even_b[e],
                                  name="even_back_sample", **geo_s)
            outs["k_sample"] = k_s.reshape(bs, 1, H_B, 2 * DH_B)
            outs["v_sample"] = v_s.reshape(bs, 1, H_B, DV_B)
            outs["conv_a_sample"] = jnp.concatenate([hist[:, 1:], u_s[:, None, :]], axis=1)
        else:
            od = layer // 2
            prompt_mix = None
            xp, pst, dst = _odd_call(xp, mp1[0], mp1[1], mp1[2], gpre, gpost, w_in_odd_b[od], w_out_odd_b[od],
                                     w_pool_b[od], pool_scale[od][None], conv_d_w[od], start_pos=0, **geo_p)
            outs["pool_prompt"] = pst[:, HALO_P - POOL_HIST:, :]
            outs["conv_d_prompt"] = dst[:, HALO_D - (CONV_D - 1):, :]
            phist, dhist = state_pool[od], state_conv_d[od]
            xs, u_s, z_s = _odd_sample_call(xs, ms1[0], ms1[1], ms1[2], gpre, gpost, w_in_odd_b[od],
                                            w_out_odd_b[od], w_pool_b[od], pool_scale[od][None], conv_d_w[od],
                                            jnp.swapaxes(phist, 0, 1), jnp.swapaxes(dhist, 0, 1),
                                            start_pos=past_len)
            outs["pool_sample"] = jnp.concatenate([phist[:, 1:], u_s[:, None, :]], axis=1)
            outs["conv_d_sample"] = jnp.concatenate([dhist[:, 1:], z_s[:, None, :]], axis=1)
        xp = ffn(xp, mp2, layer, 2, 1, geo_p, "ffn_prompt", mix=prompt_mix)
        xs = ffn(xs, ms2, layer, 2, 1, geo_s, "ffn_sample")

    stack = lambda name: outs[name][None]
    return (xp.reshape(bp, seq, d), xs.reshape(bs, 1, d),
            stack("k_prompt"), stack("v_prompt"), stack("k_sample"), stack("v_sample"),
            stack("conv_a_prompt"), stack("conv_a_sample"),
            stack("pool_prompt"), stack("pool_sample"),
            stack("conv_d_prompt"), stack("conv_d_sample"))
```
